```python
import jax, jax.numpy as jnp
from jax import lax
import numpy as np

D_MODEL = 1024
BATCH = 16
SEQ = 4096
DEPTH = 1

CHUNK = 64
Q_BLOCK = 128
EPS = 1e-6

SSD_HEAD_DIM = 64
SSD_INNER = D_MODEL
SSD_HEADS = SSD_INNER // SSD_HEAD_DIM
SSD_GROUPS = 2
SSD_HEADS_PER_GROUP = SSD_HEADS // SSD_GROUPS
SSD_STATE = 128
CONV_WIDTH = 4
CONV_DIM = SSD_INNER + 2 * SSD_GROUPS * SSD_STATE

FOX_HEAD_DIM = 64
FOX_WIDTH = D_MODEL
FOX_HEADS = FOX_WIDTH // FOX_HEAD_DIM

D_FF = 2816

IN_SIZES = (SSD_INNER, CONV_DIM, SSD_HEADS, FOX_WIDTH, FOX_WIDTH, FOX_WIDTH, FOX_HEADS, D_MODEL, D_MODEL)
W_IN_COLS = sum(IN_SIZES)
IN_SPLITS = tuple(int(sum(IN_SIZES[:i + 1])) for i in range(len(IN_SIZES) - 1))

kernel_name = "streaming_hybrid_ssd_fox_macaron"


def rms_norm(x, w):
    xf = x.astype(jnp.float32)
    y = xf * lax.rsqrt(jnp.mean(xf * xf, axis=-1, keepdims=True) + EPS)
    return (y * w.astype(jnp.float32)).astype(x.dtype)


def swiglu(h, w1, w3, w2):
    return (jax.nn.silu(h @ w1) * (h @ w3)) @ w2


def causal_dwconv(u, w, b):
    y = lax.conv_general_dilated(
        u, w, window_strides=(1,), padding=((CONV_WIDTH - 1, 0),),
        dimension_numbers=("NWC", "WIO", "NWC"), feature_group_count=u.shape[-1])
    return y + b


def ssd_chunked(x, dtA, Bm, Cm):
    a_cs = jnp.cumsum(dtA, axis=2)
    pos = jnp.arange(CHUNK)
    tril = (pos[:, None] >= pos[None, :])[None, None, :, :, None, None]
    seg = a_cs[:, :, :, None] - a_cs[:, :, None, :]
    L = jnp.exp(jnp.where(tril, seg, -jnp.inf))
    cb = jnp.einsum("bclgn,bcsgn->bclsg", Cm, Bm)
    y_diag = jnp.einsum("bclsg,bclsgr,bcsgrp->bclgrp", cb, L, x)
    decay_to_end = jnp.exp(a_cs[:, :, -1:] - a_cs)
    chunk_states = jnp.einsum("bclgn,bclgr,bclgrp->bcgrpn", Bm, decay_to_end, x)
    chunk_decay = jnp.exp(a_cs[:, :, -1])

    def step(h, inp):
        dec, st = inp
        return dec[..., None, None] * h + st, h

    b = x.shape[0]
    h0 = jnp.zeros((b, SSD_GROUPS, SSD_HEADS_PER_GROUP, SSD_HEAD_DIM, SSD_STATE), jnp.float32)
    _, h_in = lax.scan(step, h0, (jnp.moveaxis(chunk_decay, 1, 0), jnp.moveaxis(chunk_states, 1, 0)))
    h_in = jnp.moveaxis(h_in, 0, 1)
    y_off = jnp.einsum("bclgn,bcgrpn,bclgr->bclgrp", Cm, h_in, jnp.exp(a_cs))
    return y_diag + y_off


def ssd_mixer(z, xbc_raw, dt_raw, conv_w, conv_b, dt_bias, a_log, d_skip, norm_w):
    b, s, _ = z.shape
    nc = s // CHUNK
    f32 = jnp.float32
    xbc = jax.nn.silu(causal_dwconv(xbc_raw, conv_w, conv_b))
    xs, Bm, Cm = jnp.split(xbc, [SSD_INNER, SSD_INNER + SSD_GROUPS * SSD_STATE], axis=-1)
    xs = xs.astype(f32).reshape(b, nc, CHUNK, SSD_GROUPS, SSD_HEADS_PER_GROUP, SSD_HEAD_DIM)
    Bm = Bm.astype(f32).reshape(b, nc, CHUNK, SSD_GROUPS, SSD_STATE)
    Cm = Cm.astype(f32).reshape(b, nc, CHUNK, SSD_GROUPS, SSD_STATE)
    dt = jax.nn.softplus(dt_raw.astype(f32) + dt_bias.astype(f32))
    dt = dt.reshape(b, nc, CHUNK, SSD_GROUPS, SSD_HEADS_PER_GROUP)
    A = -jnp.exp(a_log.astype(f32)).reshape(SSD_GROUPS, SSD_HEADS_PER_GROUP)
    y = ssd_chunked(xs * dt[..., None], dt * A, Bm, Cm)
    y = y + d_skip.astype(f32).reshape(SSD_GROUPS, SSD_HEADS_PER_GROUP)[:, :, None] * xs
    y = y.reshape(b, s, SSD_INNER) * jax.nn.silu(z.astype(f32))
    yg = y.reshape(b, s, SSD_GROUPS, SSD_INNER // SSD_GROUPS)
    yg = yg * lax.rsqrt(jnp.mean(yg * yg, axis=-1, keepdims=True) + EPS)
    y = yg.reshape(b, s, SSD_INNER) * norm_w.astype(f32)
    return y.astype(z.dtype)


def fox_mixer(q, k, v, f_logit, b_f, q_norm_w, k_norm_w):
    b, s, _ = q.shape
    f32 = jnp.float32
    q = rms_norm(q.reshape(b, s, FOX_HEADS, FOX_HEAD_DIM), q_norm_w)
    k = rms_norm(k.reshape(b, s, FOX_HEADS, FOX_HEAD_DIM), k_norm_w)
    v = v.reshape(b, s, FOX_HEADS, FOX_HEAD_DIM)
    log_f = jax.nn.log_sigmoid(f_logit.astype(f32) + b_f.astype(f32))
    cum = jnp.transpose(jnp.cumsum(log_f, axis=1), (0, 2, 1))
    scale = FOX_HEAD_DIM ** -0.5
    outs = []
    for i in range(s // Q_BLOCK):
        q0 = i * Q_BLOCK
        kl = q0 + Q_BLOCK
        sc = jnp.einsum("bqhd,bkhd->bhqk", q[:, q0:kl], k[:, :kl]).astype(f32) * scale
        sc = sc + cum[:, :, q0:kl, None] - cum[:, :, None, :kl]
        mask = jnp.arange(kl)[None, :] <= (q0 + jnp.arange(Q_BLOCK))[:, None]
        sc = jnp.where(mask, sc, -jnp.inf)
        p = jax.nn.softmax(sc, axis=-1).astype(v.dtype)
        outs.append(jnp.einsum("bhqk,bkhd->bqhd", p, v[:, :kl]))
    return jnp.concatenate(outs, axis=1).reshape(b, s, FOX_WIDTH)


def setup_inputs(seed: int = 0) -> dict:
    key = jax.random.key(seed)
    ks = jax.random.split(key, 24)
    nrm = jax.random.normal
    uni = jax.random.uniform
    f32 = jnp.float32
    dt0 = jnp.exp(uni(ks[8], (DEPTH, SSD_HEADS), f32, np.log(1e-3), np.log(1e-1)))
    return {
        "x": nrm(ks[0], (BATCH, SEQ, D_MODEL), f32),
        "ffn1_norm": 1.0 + 0.02 * nrm(ks[1], (DEPTH, D_MODEL), f32),
        "ffn1_w1": nrm(ks[2], (DEPTH, D_MODEL, D_FF), f32) * D_MODEL ** -0.5,
        "ffn1_w3": nrm(ks[3], (DEPTH, D_MODEL, D_FF), f32) * D_MODEL ** -0.5,
        "ffn1_w2": nrm(ks[4], (DEPTH, D_FF, D_MODEL), f32) * D_FF ** -0.5,
        "mix_norm": 1.0 + 0.02 * nrm(ks[5], (DEPTH, D_MODEL), f32),
        "w_in": nrm(ks[6], (DEPTH, D_MODEL, W_IN_COLS), f32) * D_MODEL ** -0.5,
        "conv_w": nrm(ks[7], (DEPTH, CONV_WIDTH, 1, CONV_DIM), f32) * CONV_WIDTH ** -0.5,
        "conv_b": 0.02 * nrm(ks[9], (DEPTH, CONV_DIM), f32),
        "dt_bias": dt0 + jnp.log(-jnp.expm1(-dt0)),
        "a_log": jnp.log(uni(ks[10], (DEPTH, SSD_HEADS), f32, 1.0, 16.0)),
        "d_skip": 1.0 + 0.1 * nrm(ks[11], (DEPTH, SSD_HEADS), f32),
        "ssd_norm_w": 1.0 + 0.02 * nrm(ks[12], (DEPTH, SSD_INNER), f32),
        "fox_b_f": uni(ks[13], (DEPTH, FOX_HEADS), f32, 1.0, 4.0),
        "q_norm_w": 1.0 + 0.02 * nrm(ks[14], (DEPTH, FOX_HEAD_DIM), f32),
        "k_norm_w": 1.0 + 0.02 * nrm(ks[15], (DEPTH, FOX_HEAD_DIM), f32),
        "w_branch_ssd": nrm(ks[16], (DEPTH, SSD_INNER, D_MODEL), f32) * SSD_INNER ** -0.5,
        "w_branch_fox": nrm(ks[17], (DEPTH, FOX_WIDTH, D_MODEL), f32) * FOX_WIDTH ** -0.5,
        "w_out": nrm(ks[18], (DEPTH, D_MODEL, D_MODEL), f32) * D_MODEL ** -0.5,
        "ffn2_norm": 1.0 + 0.02 * nrm(ks[19], (DEPTH, D_MODEL), f32),
        "ffn2_w1": nrm(ks[20], (DEPTH, D_MODEL, D_FF), f32) * D_MODEL ** -0.5,
        "ffn2_w3": nrm(ks[21], (DEPTH, D_MODEL, D_FF), f32) * D_MODEL ** -0.5,
        "ffn2_w2": nrm(ks[22], (DEPTH, D_FF, D_MODEL), f32) * D_FF ** -0.5,
    }


def reference(x, ffn1_norm, ffn1_w1, ffn1_w3, ffn1_w2, mix_norm, w_in, conv_w, conv_b,
              dt_bias, a_log, d_skip, ssd_norm_w, fox_b_f, q_norm_w, k_norm_w,
              w_branch_ssd, w_branch_fox, w_out, ffn2_norm, ffn2_w1, ffn2_w3, ffn2_w2):
    for l in range(DEPTH):
        x = x + 0.5 * swiglu(rms_norm(x, ffn1_norm[l]), ffn1_w1[l], ffn1_w3[l], ffn1_w2[l])
        h = rms_norm(x, mix_norm[l])
        proj = h @ w_in[l]
        z, xbc, dt_raw, q, k, v, f_logit, g_ssd, g_fox = jnp.split(proj, IN_SPLITS, axis=-1)
        y_ssd = ssd_mixer(z, xbc, dt_raw, conv_w[l], conv_b[l], dt_bias[l], a_log[l],
                          d_skip[l], ssd_norm_w[l])
        y_fox = fox_mixer(q, k, v, f_logit, fox_b_f[l], q_norm_w[l], k_norm_w[l])
        merged = (jax.nn.sigmoid(g_ssd) * (y_ssd @ w_branch_ssd[l])
                  + jax.nn.sigmoid(g_fox) * (y_fox @ w_branch_fox[l]))
        x = x + merged @ w_out[l]
        x = x + 0.5 * swiglu(rms_norm(x, ffn2_norm[l]), ffn2_w1[l], ffn2_w3[l], ffn2_w2[l])
    return x
```

```python
import functools

import jax
import jax.numpy as jnp
from jax import lax
from jax.experimental import pallas as pl
from jax.experimental.pallas import tpu as pltpu

F32 = jnp.float32
BF16 = jnp.bfloat16

EPS = 1e-6
LANES = 128
HEAD_DIM = 64
SSD_GROUPS = 2
SSD_STATE = 128
CONV_WIDTH = 4
NEG_BIG = -1e30

TOKEN_TILE = 512
FF_CHUNK = 1024
ATTN_BLOCK = 256
SSD_CHUNK = 256
CUM_BLOCK = 256
VMEM_LIMIT = 56 * 1024 * 1024


def _params(*sem):
    return pltpu.CompilerParams(dimension_semantics=sem, vmem_limit_bytes=VMEM_LIMIT)


def _resident(shape):
    nd = len(shape)
    return pl.BlockSpec(shape, lambda *_: (0,) * nd, pipeline_mode=pl.Buffered(1))


def _rms(x, g):
    return x * lax.rsqrt(jnp.mean(x * x, axis=-1, keepdims=True) + EPS) * g


def _silu(x):
    return x * jax.nn.sigmoid(x)


def _softplus(x):
    return jnp.maximum(x, 0.0) + jnp.log1p(jnp.exp(-jnp.abs(x)))


def _split3(x):
    hi = x.astype(BF16).astype(F32)
    r = x - hi
    mid = r.astype(BF16).astype(F32)
    lo = (r - mid).astype(BF16).astype(F32)
    return hi, mid, lo


def _tri_cumsum(tri, x):
    hi, mid, lo = _split3(x)
    dot = functools.partial(jnp.dot, preferred_element_type=F32)
    return dot(tri, hi.astype(BF16)) + dot(tri, mid.astype(BF16)) + dot(tri, lo.astype(BF16))


def _lower_tri(n):
    r = lax.broadcasted_iota(jnp.int32, (n, n), 0)
    c = lax.broadcasted_iota(jnp.int32, (n, n), 1)
    return r >= c


def _swiglu_residual(x, g, w1_ref, w3_ref, w2_ref):
    h = _rms(x, g).astype(BF16)
    d_ff = w1_ref.shape[1]
    acc = None
    for c0 in range(0, d_ff, FF_CHUNK):
        c1 = min(c0 + FF_CHUNK, d_ff)
        a = jnp.dot(h, w1_ref[:, c0:c1], preferred_element_type=F32)
        b = jnp.dot(h, w3_ref[:, c0:c1], preferred_element_type=F32)
        gate = (_silu(a) * b).astype(BF16)
        part = jnp.dot(gate, w2_ref[c0:c1, :], preferred_element_type=F32)
        acc = part if acc is None else acc + part
    return x + 0.5 * acc


def _ffn_kernel(x_ref, g_ref, w1_ref, w3_ref, w2_ref, o_ref):
    o_ref[...] = _swiglu_residual(x_ref[...], g_ref[...], w1_ref, w3_ref, w2_ref)


def _ffn(x2d, g, w1, w3, w2):
    t, d = x2d.shape
    tile = pl.BlockSpec((TOKEN_TILE, d), lambda i: (i, 0))
    return pl.pallas_call(
        _ffn_kernel,
        out_shape=jax.ShapeDtypeStruct((t, d), F32),
        grid=(t // TOKEN_TILE,),
        in_specs=[tile, _resident(g.shape), _resident(w1.shape), _resident(w3.shape), _resident(w2.shape)],
        out_specs=tile,
        compiler_params=_params("parallel"),
        name="ffn1",
    )(x2d, g, w1, w3, w2)


def _in_proj_kernel(x_ref, g_ref, *refs):
    n = len(refs) // 2
    h = _rms(x_ref[...], g_ref[...]).astype(BF16)
    for w_ref, o_ref in zip(refs[:n], refs[n:]):
        o_ref[...] = jnp.dot(h, w_ref[...], preferred_element_type=F32).astype(o_ref.dtype)


def _in_proj(x2d, g, weights, out_dtypes):
    t, d = x2d.shape
    tile = lambda n: pl.BlockSpec((TOKEN_TILE, n), lambda i: (i, 0))
    return pl.pallas_call(
        _in_proj_kernel,
        out_shape=[jax.ShapeDtypeStruct((t, w.shape[1]), dt) for w, dt in zip(weights, out_dtypes)],
        grid=(t // TOKEN_TILE,),
        in_specs=[tile(d), _resident(g.shape)] + [_resident(w.shape) for w in weights],
        out_specs=[tile(w.shape[1]) for w in weights],
        compiler_params=_params("parallel"),
        name="in_proj",
    )(x2d, g, *weights)


def _merge_ffn_kernel(x_ref, gs_ref, gf_ref, ys_ref, yf_ref, wbs_ref, wbf_ref, wo_ref,
                      g_ref, w1_ref, w3_ref, w2_ref, o_ref):
    dot = functools.partial(jnp.dot, preferred_element_type=F32)
    merged = (jax.nn.sigmoid(gs_ref[...].astype(F32)) * dot(ys_ref[...], wbs_ref[...])
              + jax.nn.sigmoid(gf_ref[...].astype(F32)) * dot(yf_ref[...], wbf_ref[...]))
    x2 = x_ref[...] + dot(merged.astype(BF16), wo_ref[...])
    o_ref[...] = _swiglu_residual(x2, g_ref[...], w1_ref, w3_ref, w2_ref)


def _merge_ffn(x2d, gs, gf, ys, yf, wbs, wbf, wo, g, w1, w3, w2):
    t, d = x2d.shape
    tile = pl.BlockSpec((TOKEN_TILE, d), lambda i: (i, 0))
    consts = (wbs, wbf, wo, g, w1, w3, w2)
    return pl.pallas_call(
        _merge_ffn_kernel,
        out_shape=jax.ShapeDtypeStruct((t, d), F32),
        grid=(t // TOKEN_TILE,),
        in_specs=[tile] * 5 + [_resident(c.shape) for c in consts],
        out_specs=tile,
        compiler_params=_params("parallel"),
        name="merge_ffn2",
    )(x2d, gs, gf, ys, yf, *consts)


def _fox_cum_kernel(small_ref, bias_ref, o_ref):
    s = small_ref.shape[1]
    tri = _lower_tri(CUM_BLOCK).astype(BF16)
    bias = bias_ref[...]

    def step(i, carry):
        rows = pl.ds(pl.multiple_of(i * CUM_BLOCK, CUM_BLOCK), CUM_BLOCK)
        log_f = -_softplus(-(small_ref[0, rows, :] + bias))
        c = _tri_cumsum(tri, log_f) + carry
        o_ref[0, rows, :] = c
        return c[CUM_BLOCK - 1:CUM_BLOCK, :]

    lax.fori_loop(0, s // CUM_BLOCK, step, jnp.zeros((1, LANES), F32))


def _fox_cum(small, bias):
    b, s, _ = small.shape
    blk = pl.BlockSpec((1, s, LANES), lambda i: (i, 0, 0))
    return pl.pallas_call(
        _fox_cum_kernel,
        out_shape=jax.ShapeDtypeStruct((b, s, LANES), F32),
        grid=(b,),
        in_specs=[blk, _resident(bias.shape)],
        out_specs=blk,
        compiler_params=_params("parallel"),
        name="fox_cum",
    )(small, bias)


def _fox_attn_kernel(q_ref, k_ref, v_ref, cum_ref, qw_ref, kw_ref, o_ref, qa_ref, ka_ref, va_ref, *, cum_lane0):
    s = q_ref.shape[1]
    pair = pl.program_id(1)
    qi = pl.program_id(2)
    blk = ATTN_BLOCK
    prep_rows = 512

    @pl.when(qi == 0)
    def _prepare():
        qw = qw_ref[...] * (HEAD_DIM ** -0.5)
        kw = kw_ref[...]

        def prep(i, _):
            rows = pl.ds(pl.multiple_of(i * prep_rows, prep_rows), prep_rows)
            lane = lax.broadcasted_iota(jnp.int32, (prep_rows, LANES), 1)
            low = lane < HEAD_DIM

            def head_norm(x_ref, w):
                x = x_ref[0, rows, :].astype(F32)
                x2 = x * x
                tot = jnp.sum(x2, axis=-1, keepdims=True)
                lo = jnp.sum(jnp.where(low, x2, 0.0), axis=-1, keepdims=True)
                r_lo = lax.rsqrt(lo * (1.0 / HEAD_DIM) + EPS)
                r_hi = lax.rsqrt((tot - lo) * (1.0 / HEAD_DIM) + EPS)
                return x * jnp.where(low, r_lo, r_hi) * w

            qn = head_norm(q_ref, qw)
            kn = head_norm(k_ref, kw)
            vv = v_ref[0, rows, :].astype(F32)
            cum = cum_ref[0, rows, :]
            for hh in range(2):
                own = low if hh == 0 else jnp.logical_not(low)
                e0 = HEAD_DIM * (1 - hh)
                c = jnp.sum(jnp.where(lane == cum_lane0 + 2 * pair + hh, cum, 0.0), axis=-1, keepdims=True)
                hi, mid, lo3 = _split3(c)
                ones3 = (lane >= e0 + 3) & (lane < e0 + 6)
                q_ext = jnp.where(lane == e0, hi, jnp.where(lane == e0 + 1, mid, jnp.where(lane == e0 + 2, lo3,
                                  jnp.where(ones3, 1.0, 0.0))))
                k_ext = jnp.where(lane == e0 + 3, -hi, jnp.where(lane == e0 + 4, -mid, jnp.where(lane == e0 + 5, -lo3,
                                  jnp.where((lane >= e0) & (lane < e0 + 3), 1.0, 0.0))))
                v_ext = jnp.where(lane == e0, 1.0, 0.0)
                qa_ref[hh, rows, :] = jnp.where(own, qn, q_ext).astype(BF16)
                ka_ref[hh, rows, :] = jnp.where(own, kn, k_ext).astype(BF16)
                va_ref[hh, rows, :] = jnp.where(own, vv, v_ext).astype(BF16)
            return 0

        lax.fori_loop(0, s // prep_rows, prep, 0)

    q_rows = pl.ds(pl.multiple_of(qi * blk, blk), blk)
    tril = _lower_tri(blk)
    outs = []
    for hh in range(2):
        q_blk = qa_ref[hh, q_rows, :]

        def block(j, carry, masked, hh=hh, q_blk=q_blk):
            m, acc = carry
            k_rows = pl.ds(pl.multiple_of(j * blk, blk), blk)
            sc = lax.dot_general(q_blk, ka_ref[hh, k_rows, :], (((1,), (1,)), ((), ())),
                                 preferred_element_type=F32)
            if masked:
                sc = jnp.where(tril, sc, NEG_BIG)
            m_new = jnp.maximum(m, jnp.max(sc, axis=-1, keepdims=True))
            p = jnp.exp(sc - m_new).astype(BF16)
            acc = jnp.exp(m - m_new) * acc + jnp.dot(p, va_ref[hh, k_rows, :], preferred_element_type=F32)
            return m_new, acc

        init = (jnp.full((blk, 1), NEG_BIG, F32), jnp.zeros((blk, LANES), F32))
        carry = lax.fori_loop(0, qi, functools.partial(block, masked=False), init)
        _, acc = block(qi, carry, masked=True)
        e0 = HEAD_DIM * (1 - hh)
        outs.append(acc / acc[:, e0:e0 + 1])
    lane = lax.broadcasted_iota(jnp.int32, (blk, LANES), 1)
    o_ref[0] = jnp.where(lane < HEAD_DIM, outs[0], outs[1]).astype(o_ref.dtype)


def _fox_attn(q, k, v, cum, qw2, kw2, cum_lane0):
    b, s, w = q.shape
    pairs = w // LANES
    seq = pl.BlockSpec((1, s, LANES), lambda bi, pi, qi: (bi, 0, pi))
    return pl.pallas_call(
        functools.partial(_fox_attn_kernel, cum_lane0=cum_lane0),
        out_shape=jax.ShapeDtypeStruct((b, s, w), BF16),
        grid=(b, pairs, s // ATTN_BLOCK),
        in_specs=[seq, seq, seq,
                  pl.BlockSpec((1, s, LANES), lambda bi, pi, qi: (bi, 0, 0)),
                  _resident(qw2.shape), _resident(kw2.shape)],
        out_specs=pl.BlockSpec((1, ATTN_BLOCK, LANES), lambda bi, pi, qi: (bi, qi, pi)),
        scratch_shapes=[pltpu.VMEM((2, s, LANES), BF16)] * 3,
        compiler_params=_params("arbitrary", "arbitrary", "arbitrary"),
        name="fox_attn",
    )(q, k, v, cum, qw2, kw2)


def _ssd_kernel(xbc_ref, z_ref, small_ref, cw_ref, cb_ref, dtb_ref, alog_ref, dskip_ref, nw_ref, expand_ref,
                o_ref, ubuf_ref, state_ref):
    n = SSD_CHUNK
    inner = z_ref.shape[2]
    gw = inner // SSD_GROUPS
    heads_per_group = gw // HEAD_DIM
    dot = functools.partial(jnp.dot, preferred_element_type=F32)
    pad = 8

    @pl.when(pl.program_id(1) == 0)
    def _reset():
        state_ref[...] = jnp.zeros_like(state_ref)
        ubuf_ref[0:pad, :] = jnp.zeros((pad, ubuf_ref.shape[1]), F32)

    ubuf_ref[pad:pad + n, :] = xbc_ref[0].astype(F32)
    conv = cb_ref[...]
    for j in range(CONV_WIDTH):
        conv = conv + cw_ref[j:j + 1, :] * ubuf_ref[pad - (CONV_WIDTH - 1) + j:pad - (CONV_WIDTH - 1) + j + n, :]
    ubuf_ref[0:pad, :] = ubuf_ref[n:n + pad, :]
    xc = _silu(conv)
    xs = xc[:, :inner]
    b_mat = xc[:, inner:inner + SSD_GROUPS * SSD_STATE].astype(BF16)
    c_mat = xc[:, inner + SSD_GROUPS * SSD_STATE:].astype(BF16)

    lane = lax.broadcasted_iota(jnp.int32, (n, LANES), 1)
    head_lane = lane < SSD_GROUPS * heads_per_group
    dt = jnp.where(head_lane, _softplus(small_ref[0] + dtb_ref[...]), 0.0)
    a_cs = _tri_cumsum(_lower_tri(n).astype(BF16), dt * (-jnp.exp(alog_ref[...])))
    a_cs_t = a_cs.T
    a_last = a_cs[n - 1:n, :]
    decay_in = jnp.exp(a_cs)
    decay_out = jnp.exp(a_last - a_cs)

    stacked = jnp.concatenate([dt, decay_in, decay_out], axis=0)
    hi = stacked.astype(BF16)
    lo = (stacked - hi.astype(F32)).astype(BF16)
    wide = dot(hi, expand_ref[...]) + dot(lo, expand_ref[...])
    dt_w, decay_in_w, decay_out_w = wide[:n], wide[n:2 * n], wide[2 * n:]
    chunk_decay_w = decay_in_w[n - 1:n, :]

    x_dt = xs * dt_w
    x_dt_bf = x_dt.astype(BF16)
    col = lax.broadcasted_iota(jnp.int32, (n, inner), 1)
    first_half = (col % LANES) < HEAD_DIM
    x_dt_even = jnp.where(first_half, x_dt_bf, jnp.zeros_like(x_dt_bf))
    x_dt_odd = jnp.where(first_half, jnp.zeros_like(x_dt_bf), x_dt_bf)
    x_out = (x_dt * decay_out_w).astype(BF16)

    tril = _lower_tri(n)
    ys = []
    for g in range(SSD_GROUPS):
        bg = b_mat[:, g * SSD_STATE:(g + 1) * SSD_STATE]
        cg = c_mat[:, g * SSD_STATE:(g + 1) * SSD_STATE]
        cb = lax.dot_general(cg, bg, (((1,), (1,)), ((), ())), preferred_element_type=F32)
        diag = []
        for hp in range(heads_per_group // 2):
            lanes = slice(g * gw + hp * LANES, g * gw + (hp + 1) * LANES)
            acc = None
            for sub, x_src in ((0, x_dt_even), (1, x_dt_odd)):
                h = g * heads_per_group + 2 * hp + sub
                seg = a_cs[:, h:h + 1] - a_cs_t[h:h + 1, :]
                mix = (cb * jnp.exp(jnp.where(tril, seg, NEG_BIG))).astype(BF16)
                part = dot(mix, x_src[:, lanes])
                acc = part if acc is None else acc + part
            diag.append(acc)
        cols = slice(g * gw, (g + 1) * gw)
        st = state_ref[g]
        y_off = dot(cg, st.astype(BF16)) * decay_in_w[:, cols]
        new = lax.dot_general(bg, x_out[:, cols], (((0,), (0,)), ((), ())), preferred_element_type=F32)
        state_ref[g] = chunk_decay_w[:, cols] * st + new
        ys.append(jnp.concatenate(diag, axis=1) + y_off)
    y = jnp.concatenate(ys, axis=1) + dskip_ref[...] * xs
    y = y * _silu(z_ref[0].astype(F32))
    normed = []
    for g in range(SSD_GROUPS):
        yg = y[:, g * gw:(g + 1) * gw]
        normed.append(yg * lax.rsqrt(jnp.mean(yg * yg, axis=-1, keepdims=True) + EPS))
    o_ref[0] = (jnp.concatenate(normed, axis=1) * nw_ref[...]).astype(o_ref.dtype)


def _ssd(xbc, z, small, cw, cb, dtb, alog, dskip, nw, expand):
    b, s, conv_dim = xbc.shape
    inner = z.shape[2]
    step = lambda n: pl.BlockSpec((1, SSD_CHUNK, n), lambda bi, ci: (bi, ci, 0))
    consts = (cw, cb, dtb, alog, dskip, nw, expand)
    return pl.pallas_call(
        _ssd_kernel,
        out_shape=jax.ShapeDtypeStruct((b, s, inner), BF16),
        grid=(b, s // SSD_CHUNK),
        in_specs=[step(conv_dim), step(inner), step(LANES)] + [_resident(c.shape) for c in consts],
        out_specs=step(inner),
        scratch_shapes=[pltpu.VMEM((SSD_CHUNK + 8, conv_dim), F32),
                        pltpu.VMEM((SSD_GROUPS, SSD_STATE, inner // SSD_GROUPS), F32)],
        compiler_params=_params("arbitrary", "arbitrary"),
        name="ssd",
    )(xbc, z, small, *consts)


def _pad_lanes(v, offset=0):
    return jnp.zeros((1, LANES), F32).at[0, offset:offset + v.shape[0]].set(v.astype(F32))


def kernel(x, ffn1_norm, ffn1_w1, ffn1_w3, ffn1_w2, mix_norm, w_in, conv_w, conv_b, dt_bias, a_log, d_skip, ssd_norm_w, fox_b_f, q_norm_w, k_norm_w, w_branch_ssd, w_branch_fox, w_out, ffn2_norm, ffn2_w1, ffn2_w3, ffn2_w2):
    b, s, d = x.shape
    depth = ffn1_norm.shape[0]
    inner = w_branch_ssd.shape[1]
    fox_w = w_branch_fox.shape[1]
    ssd_heads = dt_bias.shape[1]
    fox_heads = fox_b_f.shape[1]
    conv_dim = conv_w.shape[-1]
    assert inner // HEAD_DIM == ssd_heads and fox_w // HEAD_DIM == fox_heads
    assert conv_dim == inner + 2 * SSD_GROUPS * SSD_STATE and conv_w.shape[1] == CONV_WIDTH
    assert (b * s) % TOKEN_TILE == 0 and s % ATTN_BLOCK == 0 and s % SSD_CHUNK == 0 and s % 512 == 0
    sizes = (inner, conv_dim, ssd_heads, fox_w, fox_w, fox_w, fox_heads, d, d)
    assert sum(sizes) == w_in.shape[2] and ssd_heads + fox_heads <= LANES
    offs = [sum(sizes[:i]) for i in range(len(sizes) + 1)]
    row = lambda v: v.astype(F32).reshape(1, -1)
    expand = (jnp.arange(LANES)[:, None] == (jnp.arange(inner) // HEAD_DIM)[None, :]).astype(BF16)

    x2d = x.reshape(b * s, d)
    for l in range(depth):
        x1 = _ffn(x2d, row(ffn1_norm[l]), ffn1_w1[l].astype(BF16), ffn1_w3[l].astype(BF16), ffn1_w2[l].astype(BF16))

        wl = w_in[l]
        seg = lambda i: wl[:, offs[i]:offs[i + 1]]
        w_small = jnp.zeros((d, LANES), F32).at[:, :ssd_heads].set(seg(2)).at[:, ssd_heads:ssd_heads + fox_heads].set(seg(6))
        big = [seg(i).astype(BF16) for i in (0, 1, 3, 4, 5, 7, 8)]
        z, xbc, q, k, v, g_ssd, g_fox, small = _in_proj(
            x1, row(mix_norm[l]), big + [w_small.astype(BF16)], [BF16] * 7 + [F32])
        small3 = small.reshape(b, s, LANES)

        cum = _fox_cum(small3, _pad_lanes(fox_b_f[l], ssd_heads))
        y_fox = _fox_attn(q.reshape(b, s, fox_w), k.reshape(b, s, fox_w), v.reshape(b, s, fox_w), cum,
                          jnp.tile(row(q_norm_w[l]), (1, 2)), jnp.tile(row(k_norm_w[l]), (1, 2)), ssd_heads)

        y_ssd = _ssd(xbc.reshape(b, s, conv_dim), z.reshape(b, s, inner), small3,
                     conv_w[l].reshape(CONV_WIDTH, conv_dim).astype(F32), row(conv_b[l]),
                     _pad_lanes(dt_bias[l]), _pad_lanes(a_log[l]),
                     jnp.repeat(row(d_skip[l]), HEAD_DIM, axis=1), row(ssd_norm_w[l]), expand)

        x2d = _merge_ffn(x1, g_ssd, g_fox, y_ssd.reshape(b * s, inner), y_fox.reshape(b * s, fox_w),
                         w_branch_ssd[l].astype(BF16), w_branch_fox[l].astype(BF16), w_out[l].astype(BF16),
                         row(ffn2_norm[l]), ffn2_w1[l].astype(BF16), ffn2_w3[l].astype(BF16), ffn2_w2[l].astype(BF16))
    return x2d.reshape(b, s, d)
```

```python
import functools
import math

import numpy as np
import jax
import jax.numpy as jnp
from jax import lax
from jax.experimental import pallas as pl
from jax.experimental.pallas import tpu as pltpu

F32 = jnp.float32
BF16 = jnp.bfloat16

EPS = 1e-6
LANES = 128
HEAD_DIM = 64
SSD_GROUPS = 2
SSD_STATE = 128
CONV_WIDTH = 4
NEG_BIG = -1e30
LOG2E = math.log2(math.e)

TOKEN_TILE = 512
FF_CHUNK = 1024
ATTN_BLOCK = 512
SSD_CHUNK = 256
CUM_BLOCK = 256
VMEM_LIMIT = 56 * 1024 * 1024


def _params(*sem):
    return pltpu.CompilerParams(dimension_semantics=sem, vmem_limit_bytes=VMEM_LIMIT)


def _resident(shape):
    nd = len(shape)
    return pl.BlockSpec(shape, lambda *_: (0,) * nd, pipeline_mode=pl.Buffered(1))


def _rms(x, g):
    return x * lax.rsqrt(jnp.mean(x * x, axis=-1, keepdims=True) + EPS) * g


def _silu(x):
    return x * jax.nn.sigmoid(x)


def _softplus(x):
    return jnp.maximum(x, 0.0) + jnp.log1p(jnp.exp(-jnp.abs(x)))


def _split3(x):
    hi = x.astype(BF16).astype(F32)
    r = x - hi
    mid = r.astype(BF16).astype(F32)
    lo = (r - mid).astype(BF16).astype(F32)
    return hi, mid, lo


def _tri_cumsum(tri, x):
    hi, mid, lo = _split3(x)
    dot = functools.partial(jnp.dot, preferred_element_type=F32)
    return dot(tri, hi.astype(BF16)) + dot(tri, mid.astype(BF16)) + dot(tri, lo.astype(BF16))


def _lower_tri(n):
    r = lax.broadcasted_iota(jnp.int32, (n, n), 0)
    c = lax.broadcasted_iota(jnp.int32, (n, n), 1)
    return r >= c


def _swiglu_residual(x, g, w1_ref, w3_ref, w2_ref):
    h = _rms(x, g).astype(BF16)
    d_ff = w1_ref.shape[1]
    acc = None
    for c0 in range(0, d_ff, FF_CHUNK):
        c1 = min(c0 + FF_CHUNK, d_ff)
        a = jnp.dot(h, w1_ref[:, c0:c1], preferred_element_type=F32)
        b = jnp.dot(h, w3_ref[:, c0:c1], preferred_element_type=F32)
        gate = (_silu(a) * b).astype(BF16)
        part = jnp.dot(gate, w2_ref[c0:c1, :], preferred_element_type=F32)
        acc = part if acc is None else acc + part
    return x + 0.5 * acc


def _ffn_kernel(x_ref, g_ref, w1_ref, w3_ref, w2_ref, o_ref):
    o_ref[...] = _swiglu_residual(x_ref[...], g_ref[...], w1_ref, w3_ref, w2_ref)


def _ffn(x2d, g, w1, w3, w2):
    t, d = x2d.shape
    tile = pl.BlockSpec((TOKEN_TILE, d), lambda i: (i, 0))
    return pl.pallas_call(
        _ffn_kernel,
        out_shape=jax.ShapeDtypeStruct((t, d), F32),
        grid=(t // TOKEN_TILE,),
        in_specs=[tile, _resident(g.shape), _resident(w1.shape), _resident(w3.shape), _resident(w2.shape)],
        out_specs=tile,
        compiler_params=_params("parallel"),
        name="ffn1",
    )(x2d, g, w1, w3, w2)


def _in_proj_kernel(x_ref, g_ref, *refs):
    n = len(refs) // 2
    h = _rms(x_ref[...], g_ref[...]).astype(BF16)
    for w_ref, o_ref in zip(refs[:n], refs[n:]):
        o_ref[...] = jnp.dot(h, w_ref[...], preferred_element_type=F32).astype(o_ref.dtype)


def _in_proj(x2d, g, weights, out_dtypes):
    t, d = x2d.shape
    tile = lambda n: pl.BlockSpec((TOKEN_TILE, n), lambda i: (i, 0))
    return pl.pallas_call(
        _in_proj_kernel,
        out_shape=[jax.ShapeDtypeStruct((t, w.shape[1]), dt) for w, dt in zip(weights, out_dtypes)],
        grid=(t // TOKEN_TILE,),
        in_specs=[tile(d), _resident(g.shape)] + [_resident(w.shape) for w in weights],
        out_specs=[tile(w.shape[1]) for w in weights],
        compiler_params=_params("parallel"),
        name="in_proj",
    )(x2d, g, *weights)


def _merge_ffn_kernel(x_ref, gs_ref, gf_ref, ys_ref, yf_ref, wbs_ref, wbf_ref, wo_ref,
                      g_ref, w1_ref, w3_ref, w2_ref, o_ref):
    dot = functools.partial(jnp.dot, preferred_element_type=F32)
    merged = (jax.nn.sigmoid(gs_ref[...].astype(F32)) * dot(ys_ref[...], wbs_ref[...])
              + jax.nn.sigmoid(gf_ref[...].astype(F32)) * dot(yf_ref[...], wbf_ref[...]))
    x2 = x_ref[...] + dot(merged.astype(BF16), wo_ref[...])
    o_ref[...] = _swiglu_residual(x2, g_ref[...], w1_ref, w3_ref, w2_ref)


def _merge_ffn(x2d, gs, gf, ys, yf, wbs, wbf, wo, g, w1, w3, w2):
    t, d = x2d.shape
    tile = pl.BlockSpec((TOKEN_TILE, d), lambda i: (i, 0))
    consts = (wbs, wbf, wo, g, w1, w3, w2)
    return pl.pallas_call(
        _merge_ffn_kernel,
        out_shape=jax.ShapeDtypeStruct((t, d), F32),
        grid=(t // TOKEN_TILE,),
        in_specs=[tile] * 5 + [_resident(c.shape) for c in consts],
        out_specs=tile,
        compiler_params=_params("parallel"),
        name="merge_ffn2",
    )(x2d, gs, gf, ys, yf, *consts)


def _fox_cum_kernel(small_ref, bias_ref, o_ref, *, lane0, heads):
    s = small_ref.shape[1]
    tri = _lower_tri(CUM_BLOCK).astype(BF16)
    bias = bias_ref[...]
    lane = lax.broadcasted_iota(jnp.int32, (CUM_BLOCK, LANES), 1)

    def step(i, carry):
        rows = pl.ds(pl.multiple_of(i * CUM_BLOCK, CUM_BLOCK), CUM_BLOCK)
        log_f = -_softplus(-(small_ref[0, rows, :] + bias))
        c = _tri_cumsum(tri, log_f) + carry
        hi, mid, lo = _split3(c * LOG2E)
        parts = jnp.where(lane < lane0 + heads, hi, jnp.where(lane < lane0 + 2 * heads, mid, lo))
        o_ref[0, rows, :] = jnp.where(lane == 0, 1.0, parts).astype(o_ref.dtype)
        return c[CUM_BLOCK - 1:CUM_BLOCK, :]

    lax.fori_loop(0, s // CUM_BLOCK, step, jnp.zeros((1, LANES), F32))


def _fox_cum(small, bias, lane0, heads):
    b, s, _ = small.shape
    blk = pl.BlockSpec((1, s, LANES), lambda i: (i, 0, 0))
    return pl.pallas_call(
        functools.partial(_fox_cum_kernel, lane0=lane0, heads=heads),
        out_shape=jax.ShapeDtypeStruct((b, s, LANES), BF16),
        grid=(b,),
        in_specs=[blk, _resident(bias.shape)],
        out_specs=blk,
        compiler_params=_params("parallel"),
        name="fox_cum",
    )(small, bias)


def _fox_attn_kernel(q_ref, k_ref, v_ref, cum_ref, pq_ref, pk_ref, qw_ref, kw_ref, o_ref,
                     qa_ref, ka_ref, va_ref, m_ref, acc_ref, s_ref, p_ref, alpha_ref):
    s = q_ref.shape[1]
    blk = ATTN_BLOCK
    nq = s // blk
    rows_of = lambda i: pl.ds(pl.multiple_of(i * blk, blk), blk)
    dot = functools.partial(jnp.dot, preferred_element_type=F32)
    dot_nt = lambda a, b: lax.dot_general(a, b, (((1,), (1,)), ((), ())), preferred_element_type=F32)

    qw = qw_ref[...] * (HEAD_DIM ** -0.5 * LOG2E)
    kw = kw_ref[...]

    def prepare(i, _):
        r = rows_of(i)
        lane = lax.broadcasted_iota(jnp.int32, (blk, LANES), 1)
        low = lane < HEAD_DIM

        def head_norm(x_ref, w):
            x = x_ref[0, r, :].astype(F32)
            x2 = x * x
            tot = jnp.sum(x2, axis=-1, keepdims=True)
            lo = jnp.sum(jnp.where(low, x2, 0.0), axis=-1, keepdims=True)
            r_lo = lax.rsqrt(lo * (1.0 / HEAD_DIM) + EPS)
            r_hi = lax.rsqrt((tot - lo) * (1.0 / HEAD_DIM) + EPS)
            return x * jnp.where(low, r_lo, r_hi) * w

        c3 = cum_ref[0, r, :]
        qn, q_ext = head_norm(q_ref, qw), dot(c3, pq_ref[0])
        kn, k_ext = head_norm(k_ref, kw), dot(c3, pk_ref[0])
        vv = v_ref[0, r, :].astype(F32)
        qa_ref[0, r, :] = jnp.where(low, qn, q_ext).astype(BF16)
        qa_ref[1, r, :] = jnp.where(low, q_ext, qn).astype(BF16)
        ka_ref[0, r, :] = jnp.where(low, kn, k_ext).astype(BF16)
        ka_ref[1, r, :] = jnp.where(low, k_ext, kn).astype(BF16)
        va_ref[0, r, :] = jnp.where(low, vv, jnp.where(lane == HEAD_DIM, 1.0, 0.0)).astype(BF16)
        va_ref[1, r, :] = jnp.where(low, jnp.where(lane == 0, 1.0, 0.0), vv).astype(BF16)
        return 0

    lax.fori_loop(0, nq, prepare, 0)

    def diagonal(i, _):
        r = rows_of(i)
        tril = _lower_tri(blk)
        for hh in range(2):
            sc = jnp.where(tril, dot_nt(qa_ref[hh, r, :], ka_ref[hh, r, :]), NEG_BIG)
            m = jnp.max(sc, axis=-1, keepdims=True)
            acc_ref[hh, r, :] = dot(jnp.exp2(sc - m).astype(BF16), va_ref[hh, r, :])
            m_ref[hh, r, :] = m
        return 0

    lax.fori_loop(0, nq, diagonal, 0)

    def scores(qb, kb):
        for hh in range(2):
            s_ref[hh] = dot_nt(qa_ref[hh, rows_of(qb), :], ka_ref[hh, rows_of(kb), :])

    def softmax(qb):
        r = rows_of(qb)
        for hh in range(2):
            sc = s_ref[hh]
            m_old = m_ref[hh, r, :]
            m_new = jnp.maximum(m_old, jnp.max(sc, axis=-1, keepdims=True))
            p_ref[hh] = jnp.exp2(sc - m_new).astype(BF16)
            alpha_ref[hh] = jnp.exp2(m_old - m_new)
            m_ref[hh, r, :] = m_new

    def accumulate(qb, kb):
        r = rows_of(qb)
        for hh in range(2):
            acc_ref[hh, r, :] = alpha_ref[hh] * acc_ref[hh, r, :] + dot(p_ref[hh], va_ref[hh, rows_of(kb), :])

    def advance(qb, kb):
        wrap = kb + 1 >= qb
        return jnp.where(wrap, qb + 1, qb), jnp.where(wrap, 0, kb + 1)

    p_ref[...] = jnp.zeros_like(p_ref)
    alpha_ref[...] = jnp.ones_like(alpha_ref)
    first = (jnp.int32(1), jnp.int32(0))
    scores(*first)

    def pipeline(_, carry):
        prev, cur = carry
        accumulate(*prev)
        softmax(cur[0])
        nxt = advance(*cur)
        scores(jnp.minimum(nxt[0], nq - 1), nxt[1])
        return cur, nxt

    last, _ = lax.fori_loop(0, nq * (nq - 1) // 2, pipeline, (first, first))
    accumulate(*last)

    def finish(i, _):
        r = rows_of(i)
        low = lax.broadcasted_iota(jnp.int32, (blk, LANES), 1) < HEAD_DIM
        a0, a1 = acc_ref[0, r, :], acc_ref[1, r, :]
        o0 = a0 * (1.0 / a0[:, HEAD_DIM:HEAD_DIM + 1])
        o1 = a1 * (1.0 / a1[:, 0:1])
        o_ref[0, r, :] = jnp.where(low, o0, o1).astype(o_ref.dtype)
        return 0

    lax.fori_loop(0, nq, finish, 0)


def _fox_attn(q, k, v, cum, pq, pk, qw2, kw2):
    b, s, w = q.shape
    blk = ATTN_BLOCK
    seq = pl.BlockSpec((1, s, LANES), lambda bi, pi: (bi, 0, pi))
    place = pl.BlockSpec((1, LANES, LANES), lambda bi, pi: (pi, 0, 0))
    return pl.pallas_call(
        _fox_attn_kernel,
        out_shape=jax.ShapeDtypeStruct((b, s, w), BF16),
        grid=(b, w // LANES),
        in_specs=[seq, seq, seq, pl.BlockSpec((1, s, LANES), lambda bi, pi: (bi, 0, 0)), place, place,
                  _resident(qw2.shape), _resident(kw2.shape)],
        out_specs=seq,
        scratch_shapes=[pltpu.VMEM((2, s, LANES), BF16), pltpu.VMEM((2, s, LANES), BF16),
                        pltpu.VMEM((2, s, LANES), BF16), pltpu.VMEM((2, s, 1), F32),
                        pltpu.VMEM((2, s, LANES), F32), pltpu.VMEM((2, blk, blk), F32),
                        pltpu.VMEM((2, blk, blk), BF16), pltpu.VMEM((2, blk, 1), F32)],
        compiler_params=_params("parallel", "parallel"),
        name="fox_attn",
    )(q, k, v, cum, pq, pk, qw2, kw2)


def _placement(pairs, lane0, heads):
    pq = np.zeros((pairs, LANES, LANES), np.float32)
    pk = np.zeros((pairs, LANES, LANES), np.float32)
    for p in range(pairs):
        for hh in range(2):
            e0 = HEAD_DIM * (1 - hh)
            for part in range(3):
                src = lane0 + part * heads + 2 * p + hh
                pq[p, src, e0 + part] = 1.0
                pq[p, 0, e0 + 3 + part] = 1.0
                pk[p, 0, e0 + part] = 1.0
                pk[p, src, e0 + 3 + part] = -1.0
    return jnp.asarray(pq, BF16), jnp.asarray(pk, BF16)


def _ssd_kernel(xbc_ref, z_ref, small_ref, cw_ref, cb_ref, dtb_ref, alog_ref, dskip_ref, nw_ref, expand_ref,
                o_ref, ubuf_ref, state_ref):
    n = SSD_CHUNK
    inner = z_ref.shape[2]
    gw = inner // SSD_GROUPS
    heads_per_group = gw // HEAD_DIM
    dot = functools.partial(jnp.dot, preferred_element_type=F32)
    pad = 8

    @pl.when(pl.program_id(1) == 0)
    def _reset():
        state_ref[...] = jnp.zeros_like(state_ref)
        ubuf_ref[0:pad, :] = jnp.zeros((pad, ubuf_ref.shape[1]), F32)

    ubuf_ref[pad:pad + n, :] = xbc_ref[0].astype(F32)
    conv = cb_ref[...]
    for j in range(CONV_WIDTH):
        conv = conv + cw_ref[j:j + 1, :] * ubuf_ref[pad - (CONV_WIDTH - 1) + j:pad - (CONV_WIDTH - 1) + j + n, :]
    ubuf_ref[0:pad, :] = ubuf_ref[n:n + pad, :]
    xc = _silu(conv)
    xs = xc[:, :inner]
    b_mat = xc[:, inner:inner + SSD_GROUPS * SSD_STATE].astype(BF16)
    c_mat = xc[:, inner + SSD_GROUPS * SSD_STATE:].astype(BF16)

    lane = lax.broadcasted_iota(jnp.int32, (n, LANES), 1)
    head_lane = lane < SSD_GROUPS * heads_per_group
    dt = jnp.where(head_lane, _softplus(small_ref[0] + dtb_ref[...]), 0.0)
    a_cs = _tri_cumsum(_lower_tri(n).astype(BF16), dt * (-jnp.exp(alog_ref[...])))
    a_cs_t = a_cs.T
    a_last = a_cs[n - 1:n, :]
    decay_in = jnp.exp(a_cs)
    decay_out = jnp.exp(a_last - a_cs)

    stacked = jnp.concatenate([dt, decay_in, decay_out], axis=0)
    hi = stacked.astype(BF16)
    lo = (stacked - hi.astype(F32)).astype(BF16)
    wide = dot(hi, expand_ref[...]) + dot(lo, expand_ref[...])
    dt_w, decay_in_w, decay_out_w = wide[:n], wide[n:2 * n], wide[2 * n:]
    chunk_decay_w = decay_in_w[n - 1:n, :]

    x_dt = xs * dt_w
    x_dt_bf = x_dt.astype(BF16)
    col = lax.broadcasted_iota(jnp.int32, (n, inner), 1)
    first_half = (col % LANES) < HEAD_DIM
    x_dt_even = jnp.where(first_half, x_dt_bf, jnp.zeros_like(x_dt_bf))
    x_dt_odd = jnp.where(first_half, jnp.zeros_like(x_dt_bf), x_dt_bf)
    x_out = (x_dt * decay_out_w).astype(BF16)

    tril = _lower_tri(n)
    ys = []
    for g in range(SSD_GROUPS):
        bg = b_mat[:, g * SSD_STATE:(g + 1) * SSD_STATE]
        cg = c_mat[:, g * SSD_STATE:(g + 1) * SSD_STATE]
        cb = lax.dot_general(cg, bg, (((1,), (1,)), ((), ())), preferred_element_type=F32)
        diag = []
        for hp in range(heads_per_group // 2):
            lanes = slice(g * gw + hp * LANES, g * gw + (hp + 1) * LANES)
            acc = None
            for sub, x_src in ((0, x_dt_even), (1, x_dt_odd)):
                h = g * heads_per_group + 2 * hp + sub
                seg = a_cs[:, h:h + 1] - a_cs_t[h:h + 1, :]
                mix = (cb * jnp.exp(jnp.where(tril, seg, NEG_BIG))).astype(BF16)
                part = dot(mix, x_src[:, lanes])
                acc = part if acc is None else acc + part
            diag.append(acc)
        cols = slice(g * gw, (g + 1) * gw)
        st = state_ref[g]
        y_off = dot(cg, st.astype(BF16)) * decay_in_w[:, cols]
        new = lax.dot_general(bg, x_out[:, cols], (((0,), (0,)), ((), ())), preferred_element_type=F32)
        state_ref[g] = chunk_decay_w[:, cols] * st + new
        ys.append(jnp.concatenate(diag, axis=1) + y_off)
    y = jnp.concatenate(ys, axis=1) + dskip_ref[...] * xs
    y = y * _silu(z_ref[0].astype(F32))
    normed = []
    for g in range(SSD_GROUPS):
        yg = y[:, g * gw:(g + 1) * gw]
        normed.append(yg * lax.rsqrt(jnp.mean(yg * yg, axis=-1, keepdims=True) + EPS))
    o_ref[0] = (jnp.concatenate(normed, axis=1) * nw_ref[...]).astype(o_ref.dtype)


def _ssd(xbc, z, small, cw, cb, dtb, alog, dskip, nw, expand):
    b, s, conv_dim = xbc.shape
    inner = z.shape[2]
    step = lambda n: pl.BlockSpec((1, SSD_CHUNK, n), lambda bi, ci: (bi, ci, 0))
    consts = (cw, cb, dtb, alog, dskip, nw, expand)
    return pl.pallas_call(
        _ssd_kernel,
        out_shape=jax.ShapeDtypeStruct((b, s, inner), BF16),
        grid=(b, s // SSD_CHUNK),
        in_specs=[step(conv_dim), step(inner), step(LANES)] + [_resident(c.shape) for c in consts],
        out_specs=step(inner),
        scratch_shapes=[pltpu.VMEM((SSD_CHUNK + 8, conv_dim), F32),
                        pltpu.VMEM((SSD_GROUPS, SSD_STATE, inner // SSD_GROUPS), F32)],
        compiler_params=_params("arbitrary", "arbitrary"),
        name="ssd",
    )(xbc, z, small, *consts)


def _pad_lanes(v, offset=0):
    return jnp.zeros((1, LANES), F32).at[0, offset:offset + v.shape[0]].set(v.astype(F32))


def kernel(x, ffn1_norm, ffn1_w1, ffn1_w3, ffn1_w2, mix_norm, w_in, conv_w, conv_b, dt_bias, a_log, d_skip, ssd_norm_w, fox_b_f, q_norm_w, k_norm_w, w_branch_ssd, w_branch_fox, w_out, ffn2_norm, ffn2_w1, ffn2_w3, ffn2_w2):
    b, s, d = x.shape
    depth = ffn1_norm.shape[0]
    inner = w_branch_ssd.shape[1]
    fox_w = w_branch_fox.shape[1]
    ssd_heads = dt_bias.shape[1]
    fox_heads = fox_b_f.shape[1]
    conv_dim = conv_w.shape[-1]
    assert inner // HEAD_DIM == ssd_heads and fox_w // HEAD_DIM == fox_heads
    assert conv_dim == inner + 2 * SSD_GROUPS * SSD_STATE and conv_w.shape[1] == CONV_WIDTH
    assert (b * s) % TOKEN_TILE == 0 and s % ATTN_BLOCK == 0 and s // ATTN_BLOCK >= 2 and s % SSD_CHUNK == 0
    assert s % CUM_BLOCK == 0
    sizes = (inner, conv_dim, ssd_heads, fox_w, fox_w, fox_w, fox_heads, d, d)
    assert sum(sizes) == w_in.shape[2] and 0 < ssd_heads and ssd_heads + 3 * fox_heads <= LANES
    offs = [sum(sizes[:i]) for i in range(len(sizes) + 1)]
    row = lambda v: v.astype(F32).reshape(1, -1)
    expand = (jnp.arange(LANES)[:, None] == (jnp.arange(inner) // HEAD_DIM)[None, :]).astype(BF16)

    x2d = x.reshape(b * s, d)
    for l in range(depth):
        x1 = _ffn(x2d, row(ffn1_norm[l]), ffn1_w1[l].astype(BF16), ffn1_w3[l].astype(BF16), ffn1_w2[l].astype(BF16))

        wl = w_in[l]
        seg = lambda i: wl[:, offs[i]:offs[i + 1]]
        w_small = jnp.zeros((d, LANES), F32).at[:, :ssd_heads].set(seg(2))
        for part in range(3):
            lo = ssd_heads + part * fox_heads
            w_small = w_small.at[:, lo:lo + fox_heads].set(seg(6))
        big = [seg(i).astype(BF16) for i in (0, 1, 3, 4, 5, 7, 8)]
        z, xbc, q, k, v, g_ssd, g_fox, small = _in_proj(
            x1, row(mix_norm[l]), big + [w_small.astype(BF16)], [BF16] * 7 + [F32])
        small3 = small.reshape(b, s, LANES)

        cum = _fox_cum(small3, _pad_lanes(jnp.tile(fox_b_f[l], 3), ssd_heads), ssd_heads, fox_heads)
        pq, pk = _placement(fox_w // LANES, ssd_heads, fox_heads)
        y_fox = _fox_attn(q.reshape(b, s, fox_w), k.reshape(b, s, fox_w), v.reshape(b, s, fox_w), cum, pq, pk,
                          jnp.tile(row(q_norm_w[l]), (1, 2)), jnp.tile(row(k_norm_w[l]), (1, 2)))

        y_ssd = _ssd(xbc.reshape(b, s, conv_dim), z.reshape(b, s, inner), small3,
                     conv_w[l].reshape(CONV_WIDTH, conv_dim).astype(F32), row(conv_b[l]),
                     _pad_lanes(dt_bias[l]), _pad_lanes(a_log[l]),
                     jnp.repeat(row(d_skip[l]), HEAD_DIM, axis=1), row(ssd_norm_w[l]), expand)

        x2d = _merge_ffn(x1, g_ssd, g_fox, y_ssd.reshape(b * s, inner), y_fox.reshape(b * s, fox_w),
                         w_branch_ssd[l].astype(BF16), w_branch_fox[l].astype(BF16), w_out[l].astype(BF16),
                         row(ffn2_norm[l]), ffn2_w1[l].astype(BF16), ffn2_w3[l].astype(BF16), ffn2_w2[l].astype(BF16))
    return x2d.reshape(b, s, d)
```

```python
import functools
import math

import numpy as np
import jax
import jax.numpy as jnp
from jax import lax
from jax.experimental import pallas as pl
from jax.experimental.pallas import tpu as pltpu

F32 = jnp.float32
BF16 = jnp.bfloat16

EPS = 1e-6
LANES = 128
HEAD_DIM = 64
SSD_GROUPS = 2
SSD_STATE = 128
CONV_WIDTH = 4
NEG_BIG = -1e30
LOG2E = math.log2(math.e)

TOKEN_TILE = 512
FF_CHUNK = 1024
ATTN_BLOCK = 512
SSD_CHUNK = 256
CUM_BLOCK = 256
VMEM_LIMIT = 56 * 1024 * 1024


def _params(*sem):
    return pltpu.CompilerParams(dimension_semantics=sem, vmem_limit_bytes=VMEM_LIMIT)


def _resident(shape):
    nd = len(shape)
    return pl.BlockSpec(shape, lambda *_: (0,) * nd, pipeline_mode=pl.Buffered(1))


def _rms(x, g):
    return x * lax.rsqrt(jnp.mean(x * x, axis=-1, keepdims=True) + EPS) * g


def _silu(x):
    return x * jax.nn.sigmoid(x)


def _softplus(x):
    return jnp.maximum(x, 0.0) + jnp.log1p(jnp.exp(-jnp.abs(x)))


def _split3(x):
    hi = x.astype(BF16).astype(F32)
    r = x - hi
    mid = r.astype(BF16).astype(F32)
    lo = (r - mid).astype(BF16).astype(F32)
    return hi, mid, lo


def _tri_cumsum(tri, x):
    hi, mid, lo = _split3(x)
    dot = functools.partial(jnp.dot, preferred_element_type=F32)
    return dot(tri, hi.astype(BF16)) + dot(tri, mid.astype(BF16)) + dot(tri, lo.astype(BF16))


def _lower_tri(n):
    r = lax.broadcasted_iota(jnp.int32, (n, n), 0)
    c = lax.broadcasted_iota(jnp.int32, (n, n), 1)
    return r >= c


def _swiglu_residual(x, g, w1_ref, w3_ref, w2_ref):
    h = _rms(x, g).astype(BF16)
    d_ff = w1_ref.shape[1]
    acc = None
    for c0 in range(0, d_ff, FF_CHUNK):
        c1 = min(c0 + FF_CHUNK, d_ff)
        a = jnp.dot(h, w1_ref[:, c0:c1], preferred_element_type=F32)
        b = jnp.dot(h, w3_ref[:, c0:c1], preferred_element_type=F32)
        gate = (_silu(a) * b).astype(BF16)
        part = jnp.dot(gate, w2_ref[c0:c1, :], preferred_element_type=F32)
        acc = part if acc is None else acc + part
    return x + 0.5 * acc


def _ffn_kernel(x_ref, g_ref, w1_ref, w3_ref, w2_ref, o_ref):
    o_ref[...] = _swiglu_residual(x_ref[...], g_ref[...], w1_ref, w3_ref, w2_ref)


def _ffn(x2d, g, w1, w3, w2):
    t, d = x2d.shape
    tile = pl.BlockSpec((TOKEN_TILE, d), lambda i: (i, 0))
    return pl.pallas_call(
        _ffn_kernel,
        out_shape=jax.ShapeDtypeStruct((t, d), F32),
        grid=(t // TOKEN_TILE,),
        in_specs=[tile, _resident(g.shape), _resident(w1.shape), _resident(w3.shape), _resident(w2.shape)],
        out_specs=tile,
        compiler_params=_params("parallel"),
        name="ffn1",
    )(x2d, g, w1, w3, w2)


def _in_proj_kernel(x_ref, g_ref, *refs):
    n = len(refs) // 2
    h = _rms(x_ref[...], g_ref[...]).astype(BF16)
    for w_ref, o_ref in zip(refs[:n], refs[n:]):
        o_ref[...] = jnp.dot(h, w_ref[...], preferred_element_type=F32).astype(o_ref.dtype)


def _in_proj(x2d, g, weights, out_dtypes):
    t, d = x2d.shape
    tile = lambda n: pl.BlockSpec((TOKEN_TILE, n), lambda i: (i, 0))
    return pl.pallas_call(
        _in_proj_kernel,
        out_shape=[jax.ShapeDtypeStruct((t, w.shape[1]), dt) for w, dt in zip(weights, out_dtypes)],
        grid=(t // TOKEN_TILE,),
        in_specs=[tile(d), _resident(g.shape)] + [_resident(w.shape) for w in weights],
        out_specs=[tile(w.shape[1]) for w in weights],
        compiler_params=_params("parallel"),
        name="in_proj",
    )(x2d, g, *weights)


def _merge_ffn_kernel(x_ref, gs_ref, gf_ref, ys_ref, yf_ref, wbs_ref, wbf_ref, wo_ref,
                      g_ref, w1_ref, w3_ref, w2_ref, o_ref):
    dot = functools.partial(jnp.dot, preferred_element_type=F32)
    merged = (jax.nn.sigmoid(gs_ref[...].astype(F32)) * dot(ys_ref[...], wbs_ref[...])
              + jax.nn.sigmoid(gf_ref[...].astype(F32)) * dot(yf_ref[...], wbf_ref[...]))
    x2 = x_ref[...] + dot(merged.astype(BF16), wo_ref[...])
    o_ref[...] = _swiglu_residual(x2, g_ref[...], w1_ref, w3_ref, w2_ref)


def _merge_ffn(x2d, gs, gf, ys, yf, wbs, wbf, wo, g, w1, w3, w2):
    t, d = x2d.shape
    tile = pl.BlockSpec((TOKEN_TILE, d), lambda i: (i, 0))
    consts = (wbs, wbf, wo, g, w1, w3, w2)
    return pl.pallas_call(
        _merge_ffn_kernel,
        out_shape=jax.ShapeDtypeStruct((t, d), F32),
        grid=(t // TOKEN_TILE,),
        in_specs=[tile] * 5 + [_resident(c.shape) for c in consts],
        out_specs=tile,
        compiler_params=_params("parallel"),
        name="merge_ffn2",
    )(x2d, gs, gf, ys, yf, *consts)


def _fox_cum_kernel(small_ref, bias_ref, o_ref, *, lane0, heads):
    s = small_ref.shape[1]
    tri = _lower_tri(CUM_BLOCK).astype(BF16)
    bias = bias_ref[...]
    lane = lax.broadcasted_iota(jnp.int32, (CUM_BLOCK, LANES), 1)

    def step(i, carry):
        rows = pl.ds(pl.multiple_of(i * CUM_BLOCK, CUM_BLOCK), CUM_BLOCK)
        log_f = -_softplus(-(small_ref[0, rows, :] + bias))
        c = _tri_cumsum(tri, log_f) + carry
        hi, mid, lo = _split3(c * LOG2E)
        parts = jnp.where(lane < lane0 + heads, hi, jnp.where(lane < lane0 + 2 * heads, mid, lo))
        o_ref[0, rows, :] = jnp.where(lane == 0, 1.0, parts).astype(o_ref.dtype)
        return c[CUM_BLOCK - 1:CUM_BLOCK, :]

    lax.fori_loop(0, s // CUM_BLOCK, step, jnp.zeros((1, LANES), F32))


def _fox_cum(small, bias, lane0, heads):
    b, s, _ = small.shape
    blk = pl.BlockSpec((1, s, LANES), lambda i: (i, 0, 0))
    return pl.pallas_call(
        functools.partial(_fox_cum_kernel, lane0=lane0, heads=heads),
        out_shape=jax.ShapeDtypeStruct((b, s, LANES), BF16),
        grid=(b,),
        in_specs=[blk, _resident(bias.shape)],
        out_specs=blk,
        compiler_params=_params("parallel"),
        name="fox_cum",
    )(small, bias)


def _fox_attn_kernel(q_ref, k_ref, v_ref, cum_ref, pq_ref, pk_ref, qw_ref, kw_ref, o_ref,
                     qa_ref, ka_ref, va_ref, m_ref, acc_ref, s_ref, p_ref, alpha_ref):
    s = q_ref.shape[1]
    blk = ATTN_BLOCK
    nq = s // blk
    rows_of = lambda i: pl.ds(pl.multiple_of(i * blk, blk), blk)
    dot = functools.partial(jnp.dot, preferred_element_type=F32)
    dot_nt = lambda a, b: lax.dot_general(a, b, (((1,), (1,)), ((), ())), preferred_element_type=F32)

    qw = qw_ref[...] * (HEAD_DIM ** -0.5 * LOG2E)
    kw = kw_ref[...]

    def prepare(i, _):
        r = rows_of(i)
        low = lax.broadcasted_iota(jnp.int32, (blk, LANES), 1) < HEAD_DIM
        same_head = ((lax.broadcasted_iota(jnp.int32, (LANES, LANES), 0) < HEAD_DIM)
                     == (lax.broadcasted_iota(jnp.int32, (LANES, LANES), 1) < HEAD_DIM)).astype(BF16)

        def head_norm(x_ref, w):
            x = x_ref[0, r, :].astype(F32)
            mean_sq = dot((x * x).astype(BF16), same_head) * (1.0 / HEAD_DIM)
            return x * lax.rsqrt(mean_sq + EPS) * w

        c3 = cum_ref[0, r, :]
        qn, q_ext = head_norm(q_ref, qw), dot(c3, pq_ref[0])
        kn, k_ext = head_norm(k_ref, kw), dot(c3, pk_ref[0])
        vv = v_ref[0, r, :].astype(F32)
        qa_ref[0, r, :] = jnp.where(low, qn, q_ext).astype(BF16)
        qa_ref[1, r, :] = jnp.where(low, q_ext, qn).astype(BF16)
        ka_ref[0, r, :] = jnp.where(low, kn, k_ext).astype(BF16)
        ka_ref[1, r, :] = jnp.where(low, k_ext, kn).astype(BF16)
        va_ref[0, r, :] = jnp.where(low, vv, 1.0).astype(BF16)
        va_ref[1, r, :] = jnp.where(low, 1.0, vv).astype(BF16)
        return 0

    lax.fori_loop(0, nq, prepare, 0)

    def scores(qb, kb):
        for hh in range(2):
            s_ref[hh] = dot_nt(qa_ref[hh, rows_of(qb), :], ka_ref[hh, rows_of(kb), :])

    def softmax_diagonal(qb):
        r = rows_of(qb)
        tril = _lower_tri(blk)
        for hh in range(2):
            sc = jnp.where(tril, s_ref[hh], NEG_BIG)
            m = jnp.max(sc, axis=-1, keepdims=True)
            p_ref[hh] = jnp.exp2(sc - m).astype(BF16)
            m_ref[hh, r, :] = m

    def softmax(qb):
        r = rows_of(qb)
        for hh in range(2):
            sc = s_ref[hh]
            m_old = m_ref[hh, r, :]
            m_new = jnp.maximum(m_old, jnp.max(sc, axis=-1, keepdims=True))
            p_ref[hh] = jnp.exp2(sc - m_new).astype(BF16)
            alpha_ref[hh] = jnp.exp2(m_old - m_new)
            m_ref[hh, r, :] = m_new

    def accumulate(qb, kb):
        r = rows_of(qb)
        for hh in range(2):
            acc_ref[hh, r, :] = alpha_ref[hh] * acc_ref[hh, r, :] + dot(p_ref[hh], va_ref[hh, rows_of(kb), :])

    def advance(qb, kb):
        wrap = kb + 1 >= qb
        return jnp.where(wrap, qb + 1, qb), jnp.where(wrap, 0, kb + 1)

    acc_ref[...] = jnp.zeros_like(acc_ref)
    alpha_ref[...] = jnp.ones_like(alpha_ref)
    scores(0, 0)
    softmax_diagonal(0)
    scores(1, 1)

    def diagonal(i, _):
        accumulate(i - 1, i - 1)
        softmax_diagonal(i)
        more = i + 1 < nq
        scores(jnp.where(more, i + 1, 1), jnp.where(more, i + 1, 0))
        return 0

    lax.fori_loop(1, nq, diagonal, 0)

    def off_diagonal(_, carry):
        prev, cur = carry
        accumulate(*prev)
        softmax(cur[0])
        nxt = advance(*cur)
        scores(jnp.minimum(nxt[0], nq - 1), nxt[1])
        return cur, nxt

    start = ((jnp.int32(nq - 1), jnp.int32(nq - 1)), (jnp.int32(1), jnp.int32(0)))
    last, _ = lax.fori_loop(0, nq * (nq - 1) // 2, off_diagonal, start)
    accumulate(*last)

    def finish(i, _):
        r = rows_of(i)
        low = lax.broadcasted_iota(jnp.int32, (blk, LANES), 1) < HEAD_DIM
        a0, a1 = acc_ref[0, r, :], acc_ref[1, r, :]
        row_sum = pltpu.roll(jnp.where(low, a1, a0), HEAD_DIM, axis=1)
        o_ref[0, r, :] = (jnp.where(low, a0, a1) / row_sum).astype(o_ref.dtype)
        return 0

    lax.fori_loop(0, nq, finish, 0)


def _fox_attn(q, k, v, cum, pq, pk, qw2, kw2):
    b, s, w = q.shape
    blk = ATTN_BLOCK
    seq = pl.BlockSpec((1, s, LANES), lambda bi, pi: (bi, 0, pi))
    place = pl.BlockSpec((1, LANES, LANES), lambda bi, pi: (pi, 0, 0))
    return pl.pallas_call(
        _fox_attn_kernel,
        out_shape=jax.ShapeDtypeStruct((b, s, w), BF16),
        grid=(b, w // LANES),
        in_specs=[seq, seq, seq, pl.BlockSpec((1, s, LANES), lambda bi, pi: (bi, 0, 0)), place, place,
                  _resident(qw2.shape), _resident(kw2.shape)],
        out_specs=seq,
        scratch_shapes=[pltpu.VMEM((2, s, LANES), BF16), pltpu.VMEM((2, s, LANES), BF16),
                        pltpu.VMEM((2, s, LANES), BF16), pltpu.VMEM((2, s, 1), F32),
                        pltpu.VMEM((2, s, LANES), F32), pltpu.VMEM((2, blk, blk), F32),
                        pltpu.VMEM((2, blk, blk), BF16), pltpu.VMEM((2, blk, 1), F32)],
        compiler_params=_params("parallel", "parallel"),
        name="fox_attn",
    )(q, k, v, cum, pq, pk, qw2, kw2)


def _placement(pairs, lane0, heads):
    pq = np.zeros((pairs, LANES, LANES), np.float32)
    pk = np.zeros((pairs, LANES, LANES), np.float32)
    for p in range(pairs):
        for hh in range(2):
            e0 = HEAD_DIM * (1 - hh)
            for part in range(3):
                src = lane0 + part * heads + 2 * p + hh
                pq[p, src, e0 + part] = 1.0
                pq[p, 0, e0 + 3 + part] = 1.0
                pk[p, 0, e0 + part] = 1.0
                pk[p, src, e0 + 3 + part] = -1.0
    return jnp.asarray(pq, BF16), jnp.asarray(pk, BF16)


def _ssd_kernel(xbc_ref, z_ref, small_ref, shift_ref, cw_ref, cb_ref, dtb_ref, alog_ref, dskip_ref, nw_ref,
                expand_ref, o_ref, tail_ref, state_ref):
    n = SSD_CHUNK
    inner = z_ref.shape[2]
    gw = inner // SSD_GROUPS
    heads_per_group = gw // HEAD_DIM
    dot = functools.partial(jnp.dot, preferred_element_type=F32)
    pad = 8
    taps = CONV_WIDTH - 1

    @pl.when(pl.program_id(1) == 0)
    def _reset():
        state_ref[...] = jnp.zeros_like(state_ref)
        tail_ref[...] = jnp.zeros_like(tail_ref)

    u_bf = xbc_ref[0]
    u = u_bf.astype(F32)
    shifted = dot(shift_ref[...], u_bf)
    conv = cb_ref[...] + cw_ref[taps:taps + 1, :] * u
    hist = None
    for j in range(1, CONV_WIDTH):
        w = cw_ref[taps - j:taps - j + 1, :]
        conv = conv + w * shifted[(j - 1) * n:j * n]
        from_prev = w * tail_ref[pad - j:2 * pad - j, :]
        hist = from_prev if hist is None else hist + from_prev
    tail_ref[0:pad, :] = u[n - pad:n]
    conv = jnp.concatenate([conv[0:pad] + hist, conv[pad:]], axis=0)
    xc = _silu(conv)
    xs = xc[:, :inner]
    b_mat = xc[:, inner:inner + SSD_GROUPS * SSD_STATE].astype(BF16)
    c_mat = xc[:, inner + SSD_GROUPS * SSD_STATE:].astype(BF16)

    lane = lax.broadcasted_iota(jnp.int32, (n, LANES), 1)
    head_lane = lane < SSD_GROUPS * heads_per_group
    dt = jnp.where(head_lane, _softplus(small_ref[0] + dtb_ref[...]), 0.0)
    a_cs = _tri_cumsum(_lower_tri(n).astype(BF16), dt * (-jnp.exp(alog_ref[...])))
    a_cs_t = a_cs.T
    a_last = a_cs[n - 1:n, :]
    decay_in = jnp.exp(a_cs)
    decay_out = jnp.exp(a_last - a_cs)

    stacked = jnp.where(jnp.concatenate([head_lane] * 3, axis=0),
                        jnp.concatenate([dt, decay_in, decay_out], axis=0), 0.0)
    hi = stacked.astype(BF16).astype(F32)
    lo = pltpu.roll(stacked - hi, SSD_GROUPS * heads_per_group, axis=1)
    wide = dot((hi + lo).astype(BF16), expand_ref[...])
    dt_w, decay_in_w, decay_out_w = wide[:n], wide[n:2 * n], wide[2 * n:]
    chunk_decay_w = decay_in_w[n - 1:n, :]

    x_dt = xs * dt_w
    x_dt_bf = x_dt.astype(BF16)
    col = lax.broadcasted_iota(jnp.int32, (n, inner), 1)
    first_half = (col % LANES) < HEAD_DIM
    x_dt_even = jnp.where(first_half, x_dt_bf, jnp.zeros_like(x_dt_bf))
    x_dt_odd = jnp.where(first_half, jnp.zeros_like(x_dt_bf), x_dt_bf)
    x_out = (x_dt * decay_out_w).astype(BF16)

    tril = _lower_tri(n)
    ys = []
    for g in range(SSD_GROUPS):
        bg = b_mat[:, g * SSD_STATE:(g + 1) * SSD_STATE]
        cg = c_mat[:, g * SSD_STATE:(g + 1) * SSD_STATE]
        cb = lax.dot_general(cg, bg, (((1,), (1,)), ((), ())), preferred_element_type=F32)
        diag = []
        for hp in range(heads_per_group // 2):
            lanes = slice(g * gw + hp * LANES, g * gw + (hp + 1) * LANES)
            acc = None
            for sub, x_src in ((0, x_dt_even), (1, x_dt_odd)):
                h = g * heads_per_group + 2 * hp + sub
                seg = a_cs[:, h:h + 1] - a_cs_t[h:h + 1, :]
                mix = (cb * jnp.exp(jnp.where(tril, seg, NEG_BIG))).astype(BF16)
                part = dot(mix, x_src[:, lanes])
                acc = part if acc is None else acc + part
            diag.append(acc)
        cols = slice(g * gw, (g + 1) * gw)
        st = state_ref[g]
        y_off = dot(cg, st.astype(BF16)) * decay_in_w[:, cols]
        new = lax.dot_general(bg, x_out[:, cols], (((0,), (0,)), ((), ())), preferred_element_type=F32)
        state_ref[g] = chunk_decay_w[:, cols] * st + new
        ys.append(jnp.concatenate(diag, axis=1) + y_off)
    y = jnp.concatenate(ys, axis=1) + dskip_ref[...] * xs
    y = y * _silu(z_ref[0].astype(F32))
    normed = []
    for g in range(SSD_GROUPS):
        yg = y[:, g * gw:(g + 1) * gw]
        normed.append(yg * lax.rsqrt(jnp.mean(yg * yg, axis=-1, keepdims=True) + EPS))
    o_ref[0] = (jnp.concatenate(normed, axis=1) * nw_ref[...]).astype(o_ref.dtype)


def _ssd(xbc, z, small, cw, cb, dtb, alog, dskip, nw, expand):
    b, s, conv_dim = xbc.shape
    inner = z.shape[2]
    step = lambda n: pl.BlockSpec((1, SSD_CHUNK, n), lambda bi, ci: (bi, ci, 0))
    shift = jnp.concatenate([jnp.eye(SSD_CHUNK, k=-j, dtype=BF16) for j in range(1, CONV_WIDTH)], axis=0)
    consts = (shift, cw, cb, dtb, alog, dskip, nw, expand)
    return pl.pallas_call(
        _ssd_kernel,
        out_shape=jax.ShapeDtypeStruct((b, s, inner), BF16),
        grid=(b, s // SSD_CHUNK),
        in_specs=[step(conv_dim), step(inner), step(LANES)] + [_resident(c.shape) for c in consts],
        out_specs=step(inner),
        scratch_shapes=[pltpu.VMEM((16, conv_dim), F32),
                        pltpu.VMEM((SSD_GROUPS, SSD_STATE, inner // SSD_GROUPS), F32)],
        compiler_params=_params("arbitrary", "arbitrary"),
        name="ssd",
    )(xbc, z, small, *consts)


def _pad_lanes(v, offset=0):
    return jnp.zeros((1, LANES), F32).at[0, offset:offset + v.shape[0]].set(v.astype(F32))


def kernel(x, ffn1_norm, ffn1_w1, ffn1_w3, ffn1_w2, mix_norm, w_in, conv_w, conv_b, dt_bias, a_log, d_skip, ssd_norm_w, fox_b_f, q_norm_w, k_norm_w, w_branch_ssd, w_branch_fox, w_out, ffn2_norm, ffn2_w1, ffn2_w3, ffn2_w2):
    b, s, d = x.shape
    depth = ffn1_norm.shape[0]
    inner = w_branch_ssd.shape[1]
    fox_w = w_branch_fox.shape[1]
    ssd_heads = dt_bias.shape[1]
    fox_heads = fox_b_f.shape[1]
    conv_dim = conv_w.shape[-1]
    assert inner // HEAD_DIM == ssd_heads and fox_w // HEAD_DIM == fox_heads
    assert conv_dim == inner + 2 * SSD_GROUPS * SSD_STATE and conv_w.shape[1] == CONV_WIDTH
    assert (b * s) % TOKEN_TILE == 0 and s % ATTN_BLOCK == 0 and s // ATTN_BLOCK >= 2 and s % SSD_CHUNK == 0
    assert s % CUM_BLOCK == 0
    sizes = (inner, conv_dim, ssd_heads, fox_w, fox_w, fox_w, fox_heads, d, d)
    assert sum(sizes) == w_in.shape[2] and 0 < ssd_heads and ssd_heads + 3 * fox_heads <= LANES
    offs = [sum(sizes[:i]) for i in range(len(sizes) + 1)]
    row = lambda v: v.astype(F32).reshape(1, -1)
    expand = ((jnp.arange(LANES)[:, None] % ssd_heads == (jnp.arange(inner) // HEAD_DIM)[None, :])
              & (jnp.arange(LANES)[:, None] < 2 * ssd_heads)).astype(BF16)

    x2d = x.reshape(b * s, d)
    for l in range(depth):
        x1 = _ffn(x2d, row(ffn1_norm[l]), ffn1_w1[l].astype(BF16), ffn1_w3[l].astype(BF16), ffn1_w2[l].astype(BF16))

        wl = w_in[l]
        seg = lambda i: wl[:, offs[i]:offs[i + 1]]
        w_small = jnp.zeros((d, LANES), F32).at[:, :ssd_heads].set(seg(2))
        for part in range(3):
            lo = ssd_heads + part * fox_heads
            w_small = w_small.at[:, lo:lo + fox_heads].set(seg(6))
        big = [seg(i).astype(BF16) for i in (0, 1, 3, 4, 5, 7, 8)]
        z, xbc, q, k, v, g_ssd, g_fox, small = _in_proj(
            x1, row(mix_norm[l]), big + [w_small.astype(BF16)], [BF16] * 7 + [F32])
        small3 = small.reshape(b, s, LANES)

        cum = _fox_cum(small3, _pad_lanes(jnp.tile(fox_b_f[l], 3), ssd_heads), ssd_heads, fox_heads)
        pq, pk = _placement(fox_w // LANES, ssd_heads, fox_heads)
        y_fox = _fox_attn(q.reshape(b, s, fox_w), k.reshape(b, s, fox_w), v.reshape(b, s, fox_w), cum, pq, pk,
                          jnp.tile(row(q_norm_w[l]), (1, 2)), jnp.tile(row(k_norm_w[l]), (1, 2)))

        y_ssd = _ssd(xbc.reshape(b, s, conv_dim), z.reshape(b, s, inner), small3,
                     conv_w[l].reshape(CONV_WIDTH, conv_dim).astype(F32), row(conv_b[l]),
                     _pad_lanes(dt_bias[l]), _pad_lanes(a_log[l]),
                     jnp.repeat(row(d_skip[l]), HEAD_DIM, axis=1), row(ssd_norm_w[l]), expand)

        x2d = _merge_ffn(x1, g_ssd, g_fox, y_ssd.reshape(b * s, inner), y_fox.reshape(b * s, fox_w),
                         w_branch_ssd[l].astype(BF16), w_branch_fox[l].astype(BF16), w_out[l].astype(BF16),
                         row(ffn2_norm[l]), ffn2_w1[l].astype(BF16), ffn2_w3[l].astype(BF16), ffn2_w2[l].astype(BF16))
    return x2d.reshape(b, s, d)
```

```python
import functools
import math

import numpy as np
import jax
import jax.numpy as jnp
from jax import lax
from jax.experimental import pallas as pl
from jax.experimental.pallas import tpu as pltpu

F32 = jnp.float32
BF16 = jnp.bfloat16

EPS = 1e-6
LANES = 128
HEAD_DIM = 64
SSD_GROUPS = 2
SSD_STATE = 128
CONV_WIDTH = 4
NEG_BIG = -1e30
LOG2E = math.log2(math.e)

TOKEN_TILE = 512
FF_CHUNK = 1024
ATTN_BLOCK = 512
V_ROWS = HEAD_DIM + 16
SSD_CHUNK = 256
CUM_BLOCK = 256
VMEM_LIMIT = 56 * 1024 * 1024


def _params(*sem):
    return pltpu.CompilerParams(dimension_semantics=sem, vmem_limit_bytes=VMEM_LIMIT)


def _resident(shape):
    nd = len(shape)
    return pl.BlockSpec(shape, lambda *_: (0,) * nd, pipeline_mode=pl.Buffered(1))


def _rms(x, g):
    return x * lax.rsqrt(jnp.mean(x * x, axis=-1, keepdims=True) + EPS) * g


def _silu(x):
    return x * jax.nn.sigmoid(x)


def _softplus(x):
    return jnp.maximum(x, 0.0) + jnp.log1p(jnp.exp(-jnp.abs(x)))


def _split3(x):
    hi = x.astype(BF16).astype(F32)
    r = x - hi
    mid = r.astype(BF16).astype(F32)
    lo = (r - mid).astype(BF16).astype(F32)
    return hi, mid, lo


def _tri_cumsum(tri, x):
    hi, mid, lo = _split3(x)
    dot = functools.partial(jnp.dot, preferred_element_type=F32)
    return dot(tri, hi.astype(BF16)) + dot(tri, mid.astype(BF16)) + dot(tri, lo.astype(BF16))


def _lower_tri(n):
    r = lax.broadcasted_iota(jnp.int32, (n, n), 0)
    c = lax.broadcasted_iota(jnp.int32, (n, n), 1)
    return r >= c


def _swiglu_residual(x, g, w1_ref, w3_ref, w2_ref):
    h = _rms(x, g).astype(BF16)
    d_ff = w1_ref.shape[1]
    acc = None
    for c0 in range(0, d_ff, FF_CHUNK):
        c1 = min(c0 + FF_CHUNK, d_ff)
        a = jnp.dot(h, w1_ref[:, c0:c1], preferred_element_type=F32)
        b = jnp.dot(h, w3_ref[:, c0:c1], preferred_element_type=F32)
        gate = (_silu(a) * b).astype(BF16)
        part = jnp.dot(gate, w2_ref[c0:c1, :], preferred_element_type=F32)
        acc = part if acc is None else acc + part
    return x + 0.5 * acc


def _ffn_kernel(x_ref, g_ref, w1_ref, w3_ref, w2_ref, o_ref):
    o_ref[...] = _swiglu_residual(x_ref[...], g_ref[...], w1_ref, w3_ref, w2_ref)


def _ffn(x2d, g, w1, w3, w2):
    t, d = x2d.shape
    tile = pl.BlockSpec((TOKEN_TILE, d), lambda i: (i, 0))
    return pl.pallas_call(
        _ffn_kernel,
        out_shape=jax.ShapeDtypeStruct((t, d), F32),
        grid=(t // TOKEN_TILE,),
        in_specs=[tile, _resident(g.shape), _resident(w1.shape), _resident(w3.shape), _resident(w2.shape)],
        out_specs=tile,
        compiler_params=_params("parallel"),
        name="ffn1",
    )(x2d, g, w1, w3, w2)


def _in_proj_kernel(x_ref, g_ref, *refs):
    n = len(refs) // 2
    h = _rms(x_ref[...], g_ref[...]).astype(BF16)
    for w_ref, o_ref in zip(refs[:n], refs[n:]):
        o_ref[...] = jnp.dot(h, w_ref[...], preferred_element_type=F32).astype(o_ref.dtype)


def _in_proj(x2d, g, weights, out_dtypes):
    t, d = x2d.shape
    tile = lambda n: pl.BlockSpec((TOKEN_TILE, n), lambda i: (i, 0))
    return pl.pallas_call(
        _in_proj_kernel,
        out_shape=[jax.ShapeDtypeStruct((t, w.shape[1]), dt) for w, dt in zip(weights, out_dtypes)],
        grid=(t // TOKEN_TILE,),
        in_specs=[tile(d), _resident(g.shape)] + [_resident(w.shape) for w in weights],
        out_specs=[tile(w.shape[1]) for w in weights],
        compiler_params=_params("parallel"),
        name="in_proj",
    )(x2d, g, *weights)


def _merge_ffn_kernel(x_ref, gs_ref, gf_ref, ys_ref, yf_ref, wbs_ref, wbf_ref, wo_ref,
                      g_ref, w1_ref, w3_ref, w2_ref, o_ref):
    dot = functools.partial(jnp.dot, preferred_element_type=F32)
    merged = (jax.nn.sigmoid(gs_ref[...].astype(F32)) * dot(ys_ref[...], wbs_ref[...])
              + jax.nn.sigmoid(gf_ref[...].astype(F32)) * dot(yf_ref[...], wbf_ref[...]))
    x2 = x_ref[...] + dot(merged.astype(BF16), wo_ref[...])
    o_ref[...] = _swiglu_residual(x2, g_ref[...], w1_ref, w3_ref, w2_ref)


def _merge_ffn(x2d, gs, gf, ys, yf, wbs, wbf, wo, g, w1, w3, w2):
    t, d = x2d.shape
    tile = pl.BlockSpec((TOKEN_TILE, d), lambda i: (i, 0))
    consts = (wbs, wbf, wo, g, w1, w3, w2)
    return pl.pallas_call(
        _merge_ffn_kernel,
        out_shape=jax.ShapeDtypeStruct((t, d), F32),
        grid=(t // TOKEN_TILE,),
        in_specs=[tile] * 5 + [_resident(c.shape) for c in consts],
        out_specs=tile,
        compiler_params=_params("parallel"),
        name="merge_ffn2",
    )(x2d, gs, gf, ys, yf, *consts)


def _fox_cum_kernel(small_ref, bias_ref, o_ref, *, lane0, heads):
    s = small_ref.shape[1]
    tri = _lower_tri(CUM_BLOCK).astype(BF16)
    bias = bias_ref[...]
    lane = lax.broadcasted_iota(jnp.int32, (CUM_BLOCK, LANES), 1)

    def step(i, carry):
        rows = pl.ds(pl.multiple_of(i * CUM_BLOCK, CUM_BLOCK), CUM_BLOCK)
        log_f = -_softplus(-(small_ref[0, rows, :] + bias))
        c = _tri_cumsum(tri, log_f) + carry
        hi, mid, lo = _split3(c * LOG2E)
        parts = jnp.where(lane < lane0 + heads, hi, jnp.where(lane < lane0 + 2 * heads, mid, lo))
        o_ref[0, rows, :] = jnp.where(lane == 0, 1.0, parts).astype(o_ref.dtype)
        return c[CUM_BLOCK - 1:CUM_BLOCK, :]

    lax.fori_loop(0, s // CUM_BLOCK, step, jnp.zeros((1, LANES), F32))


def _fox_cum(small, bias, lane0, heads):
    b, s, _ = small.shape
    blk = pl.BlockSpec((1, s, LANES), lambda i: (i, 0, 0))
    return pl.pallas_call(
        functools.partial(_fox_cum_kernel, lane0=lane0, heads=heads),
        out_shape=jax.ShapeDtypeStruct((b, s, LANES), BF16),
        grid=(b,),
        in_specs=[blk, _resident(bias.shape)],
        out_specs=blk,
        compiler_params=_params("parallel"),
        name="fox_cum",
    )(small, bias)


def _fox_attn_kernel(q_ref, k_ref, v_ref, cum_ref, pq_ref, pk_ref, qw_ref, kw_ref, o_ref,
                     qa_ref, ka_ref, vt_ref, m_ref, acc_ref, s_ref, p_ref, alpha_ref):
    s = q_ref.shape[1]
    blk = ATTN_BLOCK
    nq = s // blk
    rows_of = lambda i: pl.ds(pl.multiple_of(i * blk, blk), blk)
    dot = functools.partial(jnp.dot, preferred_element_type=F32)
    dot_nt = lambda a, b: lax.dot_general(a, b, (((1,), (1,)), ((), ())), preferred_element_type=F32)

    qw = qw_ref[...] * (HEAD_DIM ** -0.5 * LOG2E)
    kw = kw_ref[...]

    def prepare(i, _):
        r = rows_of(i)
        low = lax.broadcasted_iota(jnp.int32, (blk, LANES), 1) < HEAD_DIM
        same_head = ((lax.broadcasted_iota(jnp.int32, (LANES, LANES), 0) < HEAD_DIM)
                     == (lax.broadcasted_iota(jnp.int32, (LANES, LANES), 1) < HEAD_DIM)).astype(BF16)

        def head_norm(x_ref, w):
            x = x_ref[0, r, :].astype(F32)
            mean_sq = dot((x * x).astype(BF16), same_head) * (1.0 / HEAD_DIM)
            return x * lax.rsqrt(mean_sq + EPS) * w

        c3 = cum_ref[0, r, :]
        qn, q_ext = head_norm(q_ref, qw), dot(c3, pq_ref[0])
        kn, k_ext = head_norm(k_ref, kw), dot(c3, pk_ref[0])
        qa_ref[0, r, :] = jnp.where(low, qn, q_ext).astype(BF16)
        qa_ref[1, r, :] = jnp.where(low, q_ext, qn).astype(BF16)
        ka_ref[0, r, :] = jnp.where(low, kn, k_ext).astype(BF16)
        ka_ref[1, r, :] = jnp.where(low, k_ext, kn).astype(BF16)
        v_t = v_ref[0, r, :].astype(F32).T
        ones = jnp.ones((V_ROWS - HEAD_DIM, blk), F32)
        for hh in range(2):
            vt_ref[hh, i] = jnp.concatenate([v_t[hh * HEAD_DIM:(hh + 1) * HEAD_DIM], ones], axis=0).astype(BF16)
        return 0

    lax.fori_loop(0, nq, prepare, 0)

    def scores(qb, kb):
        for hh in range(2):
            s_ref[hh] = dot_nt(ka_ref[hh, rows_of(kb), :], qa_ref[hh, rows_of(qb), :])

    def softmax_diagonal(qb):
        causal = (lax.broadcasted_iota(jnp.int32, (blk, blk), 0) <= lax.broadcasted_iota(jnp.int32, (blk, blk), 1))
        for hh in range(2):
            sc = jnp.where(causal, s_ref[hh], NEG_BIG)
            m = jnp.max(sc, axis=0, keepdims=True)
            p_ref[hh] = jnp.exp2(sc - m).astype(BF16)
            m_ref[hh, qb] = m

    def softmax(qb):
        for hh in range(2):
            sc = s_ref[hh]
            m_old = m_ref[hh, qb]
            m_new = jnp.maximum(m_old, jnp.max(sc, axis=0, keepdims=True))
            p_ref[hh] = jnp.exp2(sc - m_new).astype(BF16)
            alpha_ref[hh] = jnp.exp2(m_old - m_new)
            m_ref[hh, qb] = m_new

    def accumulate(qb, kb):
        for hh in range(2):
            acc_ref[hh, qb] = alpha_ref[hh] * acc_ref[hh, qb] + dot(vt_ref[hh, kb], p_ref[hh])

    def advance(qb, kb):
        wrap = kb + 1 >= qb
        return jnp.where(wrap, qb + 1, qb), jnp.where(wrap, 0, kb + 1)

    acc_ref[...] = jnp.zeros_like(acc_ref)
    alpha_ref[...] = jnp.ones_like(alpha_ref)
    scores(0, 0)
    softmax_diagonal(0)
    scores(1, 1)

    def diagonal(i, _):
        accumulate(i - 1, i - 1)
        softmax_diagonal(i)
        more = i + 1 < nq
        scores(jnp.where(more, i + 1, 1), jnp.where(more, i + 1, 0))
        return 0

    lax.fori_loop(1, nq, diagonal, 0)

    def off_diagonal(_, carry):
        prev, cur = carry
        accumulate(*prev)
        softmax(cur[0])
        nxt = advance(*cur)
        scores(jnp.minimum(nxt[0], nq - 1), nxt[1])
        return cur, nxt

    start = ((jnp.int32(nq - 1), jnp.int32(nq - 1)), (jnp.int32(1), jnp.int32(0)))
    last, _ = lax.fori_loop(0, nq * (nq - 1) // 2, off_diagonal, start)
    accumulate(*last)

    def finish(i, _):
        heads = [acc_ref[hh, i][0:HEAD_DIM] / acc_ref[hh, i][HEAD_DIM:HEAD_DIM + 1] for hh in range(2)]
        o_ref[0, rows_of(i), :] = jnp.concatenate(heads, axis=0).T.astype(o_ref.dtype)
        return 0

    lax.fori_loop(0, nq, finish, 0)


def _fox_attn(q, k, v, cum, pq, pk, qw2, kw2):
    b, s, w = q.shape
    blk = ATTN_BLOCK
    seq = pl.BlockSpec((1, s, LANES), lambda bi, pi: (bi, 0, pi))
    place = pl.BlockSpec((1, LANES, LANES), lambda bi, pi: (pi, 0, 0))
    return pl.pallas_call(
        _fox_attn_kernel,
        out_shape=jax.ShapeDtypeStruct((b, s, w), BF16),
        grid=(b, w // LANES),
        in_specs=[seq, seq, seq, pl.BlockSpec((1, s, LANES), lambda bi, pi: (bi, 0, 0)), place, place,
                  _resident(qw2.shape), _resident(kw2.shape)],
        out_specs=seq,
        scratch_shapes=[pltpu.VMEM((2, s, LANES), BF16), pltpu.VMEM((2, s, LANES), BF16),
                        pltpu.VMEM((2, s // blk, V_ROWS, blk), BF16), pltpu.VMEM((2, s // blk, 1, blk), F32),
                        pltpu.VMEM((2, s // blk, V_ROWS, blk), F32), pltpu.VMEM((2, blk, blk), F32),
                        pltpu.VMEM((2, blk, blk), BF16), pltpu.VMEM((2, 1, blk), F32)],
        compiler_params=_params("parallel", "parallel"),
        name="fox_attn",
    )(q, k, v, cum, pq, pk, qw2, kw2)


def _placement(pairs, lane0, heads):
    pq = np.zeros((pairs, LANES, LANES), np.float32)
    pk = np.zeros((pairs, LANES, LANES), np.float32)
    for p in range(pairs):
        for hh in range(2):
            e0 = HEAD_DIM * (1 - hh)
            for part in range(3):
                src = lane0 + part * heads + 2 * p + hh
                pq[p, src, e0 + part] = 1.0
                pq[p, 0, e0 + 3 + part] = 1.0
                pk[p, 0, e0 + part] = 1.0
                pk[p, src, e0 + 3 + part] = -1.0
    return jnp.asarray(pq, BF16), jnp.asarray(pk, BF16)


def _ssd_kernel(xbc_ref, z_ref, small_ref, shift_ref, cw_ref, cb_ref, dtb_ref, alog_ref, dskip_ref, nw_ref,
                expand_ref, o_ref, tail_ref, state_ref):
    n = SSD_CHUNK
    inner = z_ref.shape[2]
    gw = inner // SSD_GROUPS
    heads_per_group = gw // HEAD_DIM
    dot = functools.partial(jnp.dot, preferred_element_type=F32)
    pad = 8
    taps = CONV_WIDTH - 1

    @pl.when(pl.program_id(1) == 0)
    def _reset():
        state_ref[...] = jnp.zeros_like(state_ref)
        tail_ref[...] = jnp.zeros_like(tail_ref)

    u_bf = xbc_ref[0]
    u = u_bf.astype(F32)
    shifted = dot(shift_ref[...], u_bf)
    conv = cb_ref[...] + cw_ref[taps:taps + 1, :] * u
    hist = None
    for j in range(1, CONV_WIDTH):
        w = cw_ref[taps - j:taps - j + 1, :]
        conv = conv + w * shifted[(j - 1) * n:j * n]
        from_prev = w * tail_ref[pad - j:2 * pad - j, :]
        hist = from_prev if hist is None else hist + from_prev
    tail_ref[0:pad, :] = u[n - pad:n]
    conv = jnp.concatenate([conv[0:pad] + hist, conv[pad:]], axis=0)
    xc = _silu(conv)
    xs = xc[:, :inner]
    b_mat = xc[:, inner:inner + SSD_GROUPS * SSD_STATE].astype(BF16)
    c_mat = xc[:, inner + SSD_GROUPS * SSD_STATE:].astype(BF16)

    lane = lax.broadcasted_iota(jnp.int32, (n, LANES), 1)
    head_lane = lane < SSD_GROUPS * heads_per_group
    dt = jnp.where(head_lane, _softplus(small_ref[0] + dtb_ref[...]), 0.0)
    a_cs = _tri_cumsum(_lower_tri(n).astype(BF16), dt * (-jnp.exp(alog_ref[...])))
    a_cs_t = a_cs.T
    a_last = a_cs[n - 1:n, :]
    decay_in = jnp.exp(a_cs)
    decay_out = jnp.exp(a_last - a_cs)

    stacked = jnp.where(jnp.concatenate([head_lane] * 3, axis=0),
                        jnp.concatenate([dt, decay_in, decay_out], axis=0), 0.0)
    hi = stacked.astype(BF16).astype(F32)
    lo = pltpu.roll(stacked - hi, SSD_GROUPS * heads_per_group, axis=1)
    wide = dot((hi + lo).astype(BF16), expand_ref[...])
    dt_w, decay_in_w, decay_out_w = wide[:n], wide[n:2 * n], wide[2 * n:]
    chunk_decay_w = decay_in_w[n - 1:n, :]

    x_dt = xs * dt_w
    x_dt_bf = x_dt.astype(BF16)
    col = lax.broadcasted_iota(jnp.int32, (n, inner), 1)
    first_half = (col % LANES) < HEAD_DIM
    x_dt_even = jnp.where(first_half, x_dt_bf, jnp.zeros_like(x_dt_bf))
    x_dt_odd = jnp.where(first_half, jnp.zeros_like(x_dt_bf), x_dt_bf)
    x_out = (x_dt * decay_out_w).astype(BF16)

    tril = _lower_tri(n)
    ys = []
    for g in range(SSD_GROUPS):
        bg = b_mat[:, g * SSD_STATE:(g + 1) * SSD_STATE]
        cg = c_mat[:, g * SSD_STATE:(g + 1) * SSD_STATE]
        cb = lax.dot_general(cg, bg, (((1,), (1,)), ((), ())), preferred_element_type=F32)
        diag = []
        for hp in range(heads_per_group // 2):
            lanes = slice(g * gw + hp * LANES, g * gw + (hp + 1) * LANES)
            acc = None
            for sub, x_src in ((0, x_dt_even), (1, x_dt_odd)):
                h = g * heads_per_group + 2 * hp + sub
                seg = a_cs[:, h:h + 1] - a_cs_t[h:h + 1, :]
                mix = (cb * jnp.exp(jnp.where(tril, seg, NEG_BIG))).astype(BF16)
                part = dot(mix, x_src[:, lanes])
                acc = part if acc is None else acc + part
            diag.append(acc)
        cols = slice(g * gw, (g + 1) * gw)
        st = state_ref[g]
        y_off = dot(cg, st.astype(BF16)) * decay_in_w[:, cols]
        new = lax.dot_general(bg, x_out[:, cols], (((0,), (0,)), ((), ())), preferred_element_type=F32)
        state_ref[g] = chunk_decay_w[:, cols] * st + new
        ys.append(jnp.concatenate(diag, axis=1) + y_off)
    y = jnp.concatenate(ys, axis=1) + dskip_ref[...] * xs
    y = y * _silu(z_ref[0].astype(F32))
    normed = []
    for g in range(SSD_GROUPS):
        yg = y[:, g * gw:(g + 1) * gw]
        normed.append(yg * lax.rsqrt(jnp.mean(yg * yg, axis=-1, keepdims=True) + EPS))
    o_ref[0] = (jnp.concatenate(normed, axis=1) * nw_ref[...]).astype(o_ref.dtype)


def _ssd(xbc, z, small, cw, cb, dtb, alog, dskip, nw, expand):
    b, s, conv_dim = xbc.shape
    inner = z.shape[2]
    step = lambda n: pl.BlockSpec((1, SSD_CHUNK, n), lambda bi, ci: (bi, ci, 0))
    shift = jnp.concatenate([jnp.eye(SSD_CHUNK, k=-j, dtype=BF16) for j in range(1, CONV_WIDTH)], axis=0)
    consts = (shift, cw, cb, dtb, alog, dskip, nw, expand)
    return pl.pallas_call(
        _ssd_kernel,
        out_shape=jax.ShapeDtypeStruct((b, s, inner), BF16),
        grid=(b, s // SSD_CHUNK),
        in_specs=[step(conv_dim), step(inner), step(LANES)] + [_resident(c.shape) for c in consts],
        out_specs=step(inner),
        scratch_shapes=[pltpu.VMEM((16, conv_dim), F32),
                        pltpu.VMEM((SSD_GROUPS, SSD_STATE, inner // SSD_GROUPS), F32)],
        compiler_params=_params("arbitrary", "arbitrary"),
        name="ssd",
    )(xbc, z, small, *consts)


def _pad_lanes(v, offset=0):
    return jnp.zeros((1, LANES), F32).at[0, offset:offset + v.shape[0]].set(v.astype(F32))


def kernel(x, ffn1_norm, ffn1_w1, ffn1_w3, ffn1_w2, mix_norm, w_in, conv_w, conv_b, dt_bias, a_log, d_skip, ssd_norm_w, fox_b_f, q_norm_w, k_norm_w, w_branch_ssd, w_branch_fox, w_out, ffn2_norm, ffn2_w1, ffn2_w3, ffn2_w2):
    b, s, d = x.shape
    depth = ffn1_norm.shape[0]
    inner = w_branch_ssd.shape[1]
    fox_w = w_branch_fox.shape[1]
    ssd_heads = dt_bias.shape[1]
    fox_heads = fox_b_f.shape[1]
    conv_dim = conv_w.shape[-1]
    assert inner // HEAD_DIM == ssd_heads and fox_w // HEAD_DIM == fox_heads
    assert conv_dim == inner + 2 * SSD_GROUPS * SSD_STATE and conv_w.shape[1] == CONV_WIDTH
    assert (b * s) % TOKEN_TILE == 0 and s % ATTN_BLOCK == 0 and s // ATTN_BLOCK >= 2 and s % SSD_CHUNK == 0
    assert s % CUM_BLOCK == 0
    sizes = (inner, conv_dim, ssd_heads, fox_w, fox_w, fox_w, fox_heads, d, d)
    assert sum(sizes) == w_in.shape[2] and 0 < ssd_heads and ssd_heads + 3 * fox_heads <= LANES
    offs = [sum(sizes[:i]) for i in range(len(sizes) + 1)]
    row = lambda v: v.astype(F32).reshape(1, -1)
    expand = ((jnp.arange(LANES)[:, None] % ssd_heads == (jnp.arange(inner) // HEAD_DIM)[None, :])
              & (jnp.arange(LANES)[:, None] < 2 * ssd_heads)).astype(BF16)

    x2d = x.reshape(b * s, d)
    for l in range(depth):
        x1 = _ffn(x2d, row(ffn1_norm[l]), ffn1_w1[l].astype(BF16), ffn1_w3[l].astype(BF16), ffn1_w2[l].astype(BF16))

        wl = w_in[l]
        seg = lambda i: wl[:, offs[i]:offs[i + 1]]
        w_small = jnp.zeros((d, LANES), F32).at[:, :ssd_heads].set(seg(2))
        for part in range(3):
            lo = ssd_heads + part * fox_heads
            w_small = w_small.at[:, lo:lo + fox_heads].set(seg(6))
        big = [seg(i).astype(BF16) for i in (0, 1, 3, 4, 5, 7, 8)]
        z, xbc, q, k, v, g_ssd, g_fox, small = _in_proj(
            x1, row(mix_norm[l]), big + [w_small.astype(BF16)], [BF16] * 7 + [F32])
        small3 = small.reshape(b, s, LANES)

        cum = _fox_cum(small3, _pad_lanes(jnp.tile(fox_b_f[l], 3), ssd_heads), ssd_heads, fox_heads)
        pq, pk = _placement(fox_w // LANES, ssd_heads, fox_heads)
        y_fox = _fox_attn(q.reshape(b, s, fox_w), k.reshape(b, s, fox_w), v.reshape(b, s, fox_w), cum, pq, pk,
                          jnp.tile(row(q_norm_w[l]), (1, 2)), jnp.tile(row(k_norm_w[l]), (1, 2)))

        y_ssd = _ssd(xbc.reshape(b, s, conv_dim), z.reshape(b, s, inner), small3,
                     conv_w[l].reshape(CONV_WIDTH, conv_dim).astype(F32), row(conv_b[l]),
                     _pad_lanes(dt_bias[l]), _pad_lanes(a_log[l]),
                     jnp.repeat(row(d_skip[l]), HEAD_DIM, axis=1), row(ssd_norm_w[l]), expand)

        x2d = _merge_ffn(x1, g_ssd, g_fox, y_ssd.reshape(b * s, inner), y_fox.reshape(b * s, fox_w),
                         w_branch_ssd[l].astype(BF16), w_branch_fox[l].astype(BF16), w_out[l].astype(BF16),
                         row(ffn2_norm[l]), ffn2_w1[l].astype(BF16), ffn2_w3[l].astype(BF16), ffn2_w2[l].astype(BF16))
    return x2d.reshape(b, s, d)
```

```python
import functools
import math

import numpy as np
import jax
import jax.numpy as jnp
from jax import lax
from jax.experimental import pallas as pl
from jax.experimental.pallas import tpu as pltpu

F32 = jnp.float32
BF16 = jnp.bfloat16

EPS = 1e-6
LANES = 128
HEAD_DIM = 64
SSD_GROUPS = 2
SSD_STATE = 128
CONV_WIDTH = 4
NEG_BIG = -1e30
LOG2E = math.log2(math.e)

TOKEN_TILE = 512
FF_CHUNK = 1024
ATTN_BLOCK = 512
V_ROWS = HEAD_DIM + 16
SSD_CHUNK = 256
CUM_BLOCK = 256
VMEM_LIMIT = 56 * 1024 * 1024


def _params(*sem):
    return pltpu.CompilerParams(dimension_semantics=sem, vmem_limit_bytes=VMEM_LIMIT)


def _resident(shape):
    nd = len(shape)
    return pl.BlockSpec(shape, lambda *_: (0,) * nd, pipeline_mode=pl.Buffered(1))


def _rms(x, g):
    return x * lax.rsqrt(jnp.mean(x * x, axis=-1, keepdims=True) + EPS) * g


def _silu(x):
    return x * jax.nn.sigmoid(x)


def _softplus(x):
    return jnp.maximum(x, 0.0) + jnp.log1p(jnp.exp(-jnp.abs(x)))


def _split3(x):
    hi = x.astype(BF16).astype(F32)
    r = x - hi
    mid = r.astype(BF16).astype(F32)
    lo = (r - mid).astype(BF16).astype(F32)
    return hi, mid, lo


def _tri_cumsum(tri, x):
    hi, mid, lo = _split3(x)
    dot = functools.partial(jnp.dot, preferred_element_type=F32)
    return dot(tri, hi.astype(BF16)) + dot(tri, mid.astype(BF16)) + dot(tri, lo.astype(BF16))


def _lower_tri(n):
    r = lax.broadcasted_iota(jnp.int32, (n, n), 0)
    c = lax.broadcasted_iota(jnp.int32, (n, n), 1)
    return r >= c


def _swiglu_residual(x, g, w1_ref, w3_ref, w2_ref):
    h = _rms(x, g).astype(BF16)
    d_ff = w1_ref.shape[1]
    acc = None
    for c0 in range(0, d_ff, FF_CHUNK):
        c1 = min(c0 + FF_CHUNK, d_ff)
        a = jnp.dot(h, w1_ref[:, c0:c1], preferred_element_type=F32)
        b = jnp.dot(h, w3_ref[:, c0:c1], preferred_element_type=F32)
        gate = (_silu(a) * b).astype(BF16)
        part = jnp.dot(gate, w2_ref[c0:c1, :], preferred_element_type=F32)
        acc = part if acc is None else acc + part
    return x + 0.5 * acc


def _ffn_kernel(x_ref, g_ref, w1_ref, w3_ref, w2_ref, o_ref):
    o_ref[...] = _swiglu_residual(x_ref[...], g_ref[...], w1_ref, w3_ref, w2_ref)


def _ffn(x2d, g, w1, w3, w2):
    t, d = x2d.shape
    tile = pl.BlockSpec((TOKEN_TILE, d), lambda i: (i, 0))
    return pl.pallas_call(
        _ffn_kernel,
        out_shape=jax.ShapeDtypeStruct((t, d), F32),
        grid=(t // TOKEN_TILE,),
        in_specs=[tile, _resident(g.shape), _resident(w1.shape), _resident(w3.shape), _resident(w2.shape)],
        out_specs=tile,
        compiler_params=_params("parallel"),
        name="ffn1",
    )(x2d, g, w1, w3, w2)


def _in_proj_kernel(x_ref, g_ref, *refs):
    n = len(refs) // 2
    h = _rms(x_ref[...], g_ref[...]).astype(BF16)
    for w_ref, o_ref in zip(refs[:n], refs[n:]):
        res = jnp.dot(h, w_ref[...], preferred_element_type=F32).astype(o_ref.dtype)
        if len(o_ref.shape) == 2:
            o_ref[...] = res
        else:
            for p in range(o_ref.shape[1]):
                o_ref[0, p] = res[:, p * LANES:(p + 1) * LANES]


def _column_block_spec(tiles_per_seq, blocks):
    return pl.BlockSpec((1, blocks, TOKEN_TILE, LANES), lambda i: (i // tiles_per_seq, 0, i % tiles_per_seq, 0))


def _in_proj(x2d, g, weights, out_dtypes, blocked, batch):
    t, d = x2d.shape
    seq = t // batch
    tile = lambda n: pl.BlockSpec((TOKEN_TILE, n), lambda i: (i, 0))
    shapes, specs = [], []
    for w, dt, blk in zip(weights, out_dtypes, blocked):
        n = w.shape[1]
        shapes.append(jax.ShapeDtypeStruct((batch, n // LANES, seq, LANES) if blk else (t, n), dt))
        specs.append(_column_block_spec(seq // TOKEN_TILE, n // LANES) if blk else tile(n))
    return pl.pallas_call(
        _in_proj_kernel,
        out_shape=shapes,
        grid=(t // TOKEN_TILE,),
        in_specs=[tile(d), _resident(g.shape)] + [_resident(w.shape) for w in weights],
        out_specs=specs,
        compiler_params=_params("parallel"),
        name="in_proj",
    )(x2d, g, *weights)


def _merge_ffn_kernel(x_ref, gs_ref, gf_ref, ys_ref, yf_ref, wbs_ref, wbf_ref, wo_ref,
                      g_ref, w1_ref, w3_ref, w2_ref, o_ref):
    dot = functools.partial(jnp.dot, preferred_element_type=F32)
    y_fox = jnp.concatenate([yf_ref[0, p] for p in range(yf_ref.shape[1])], axis=1)
    merged = (jax.nn.sigmoid(gs_ref[...].astype(F32)) * dot(ys_ref[...], wbs_ref[...])
              + jax.nn.sigmoid(gf_ref[...].astype(F32)) * dot(y_fox, wbf_ref[...]))
    x2 = x_ref[...] + dot(merged.astype(BF16), wo_ref[...])
    o_ref[...] = _swiglu_residual(x2, g_ref[...], w1_ref, w3_ref, w2_ref)


def _merge_ffn(x2d, gs, gf, ys, yf, wbs, wbf, wo, g, w1, w3, w2):
    t, d = x2d.shape
    tile = pl.BlockSpec((TOKEN_TILE, d), lambda i: (i, 0))
    consts = (wbs, wbf, wo, g, w1, w3, w2)
    return pl.pallas_call(
        _merge_ffn_kernel,
        out_shape=jax.ShapeDtypeStruct((t, d), F32),
        grid=(t // TOKEN_TILE,),
        in_specs=[tile] * 4 + [_column_block_spec(yf.shape[2] // TOKEN_TILE, yf.shape[1])]
                 + [_resident(c.shape) for c in consts],
        out_specs=tile,
        compiler_params=_params("parallel"),
        name="merge_ffn2",
    )(x2d, gs, gf, ys, yf, *consts)


def _fox_cum_kernel(small_ref, bias_ref, o_ref, *, lane0, heads):
    s = small_ref.shape[1]
    tri = _lower_tri(CUM_BLOCK).astype(BF16)
    bias = bias_ref[...]
    lane = lax.broadcasted_iota(jnp.int32, (CUM_BLOCK, LANES), 1)

    def step(i, carry):
        rows = pl.ds(pl.multiple_of(i * CUM_BLOCK, CUM_BLOCK), CUM_BLOCK)
        log_f = -_softplus(-(small_ref[0, rows, :] + bias))
        c = _tri_cumsum(tri, log_f) + carry
        hi, mid, lo = _split3(c * LOG2E)
        parts = jnp.where(lane < lane0 + heads, hi, jnp.where(lane < lane0 + 2 * heads, mid, lo))
        o_ref[0, rows, :] = jnp.where(lane == 0, 1.0, parts).astype(o_ref.dtype)
        return c[CUM_BLOCK - 1:CUM_BLOCK, :]

    lax.fori_loop(0, s // CUM_BLOCK, step, jnp.zeros((1, LANES), F32))


def _fox_cum(small, bias, lane0, heads):
    b, s, _ = small.shape
    blk = pl.BlockSpec((1, s, LANES), lambda i: (i, 0, 0))
    return pl.pallas_call(
        functools.partial(_fox_cum_kernel, lane0=lane0, heads=heads),
        out_shape=jax.ShapeDtypeStruct((b, s, LANES), BF16),
        grid=(b,),
        in_specs=[blk, _resident(bias.shape)],
        out_specs=blk,
        compiler_params=_params("parallel"),
        name="fox_cum",
    )(small, bias)


def _fox_attn_kernel(q_ref, k_ref, v_ref, cum_ref, pq_ref, pk_ref, qw_ref, kw_ref, o_ref,
                     qa_ref, ka_ref, vt_ref, m_ref, acc_ref, s_ref, p_ref, alpha_ref, smax_ref):
    s = q_ref.shape[2]
    blk = ATTN_BLOCK
    nq = s // blk
    rows_of = lambda i: pl.ds(pl.multiple_of(i * blk, blk), blk)
    dot = functools.partial(jnp.dot, preferred_element_type=F32)
    dot_nt = lambda a, b: lax.dot_general(a, b, (((1,), (1,)), ((), ())), preferred_element_type=F32)

    qw = qw_ref[...] * (HEAD_DIM ** -0.5 * LOG2E)
    kw = kw_ref[...]

    def prepare(i, _):
        r = rows_of(i)
        low = lax.broadcasted_iota(jnp.int32, (blk, LANES), 1) < HEAD_DIM
        same_head = ((lax.broadcasted_iota(jnp.int32, (LANES, LANES), 0) < HEAD_DIM)
                     == (lax.broadcasted_iota(jnp.int32, (LANES, LANES), 1) < HEAD_DIM)).astype(BF16)

        def head_norm(x_ref, w):
            x = x_ref[0, 0, r, :].astype(F32)
            mean_sq = dot((x * x).astype(BF16), same_head) * (1.0 / HEAD_DIM)
            return x * lax.rsqrt(mean_sq + EPS) * w

        c3 = cum_ref[0, r, :]
        qn, q_ext = head_norm(q_ref, qw), dot(c3, pq_ref[0])
        kn, k_ext = head_norm(k_ref, kw), dot(c3, pk_ref[0])
        qa_ref[0, r, :] = jnp.where(low, qn, q_ext).astype(BF16)
        qa_ref[1, r, :] = jnp.where(low, q_ext, qn).astype(BF16)
        ka_ref[0, r, :] = jnp.where(low, kn, k_ext).astype(BF16)
        ka_ref[1, r, :] = jnp.where(low, k_ext, kn).astype(BF16)
        v_t = v_ref[0, 0, r, :].astype(F32).T
        ones = jnp.ones((V_ROWS - HEAD_DIM, blk), F32)
        for hh in range(2):
            vt_ref[hh, i] = jnp.concatenate([v_t[hh * HEAD_DIM:(hh + 1) * HEAD_DIM], ones], axis=0).astype(BF16)
        return 0

    lax.fori_loop(0, nq, prepare, 0)

    def scores(qb, kb):
        for hh in range(2):
            sc = dot_nt(ka_ref[hh, rows_of(kb), :], qa_ref[hh, rows_of(qb), :])
            s_ref[hh] = sc
            smax_ref[hh] = jnp.max(sc, axis=0, keepdims=True)

    def softmax_diagonal(qb):
        causal = (lax.broadcasted_iota(jnp.int32, (blk, blk), 0) <= lax.broadcasted_iota(jnp.int32, (blk, blk), 1))
        for hh in range(2):
            sc = jnp.where(causal, s_ref[hh], NEG_BIG)
            m = jnp.max(sc, axis=0, keepdims=True)
            p_ref[hh] = jnp.exp2(sc - m).astype(BF16)
            m_ref[hh, qb] = m

    def softmax(qb):
        for hh in range(2):
            sc = s_ref[hh]
            m_old = m_ref[hh, qb]
            m_new = jnp.maximum(m_old, smax_ref[hh])
            p_ref[hh] = jnp.exp2(sc - m_new).astype(BF16)
            alpha_ref[hh] = jnp.exp2(m_old - m_new)
            m_ref[hh, qb] = m_new

    def accumulate(qb, kb):
        for hh in range(2):
            acc_ref[hh, qb] = alpha_ref[hh] * acc_ref[hh, qb] + dot(vt_ref[hh, kb], p_ref[hh])

    def advance(qb, kb):
        wrap = kb + 1 >= qb
        return jnp.where(wrap, qb + 1, qb), jnp.where(wrap, 0, kb + 1)

    acc_ref[...] = jnp.zeros_like(acc_ref)
    alpha_ref[...] = jnp.ones_like(alpha_ref)
    scores(0, 0)
    softmax_diagonal(0)
    scores(1, 1)

    def diagonal(i, _):
        accumulate(i - 1, i - 1)
        softmax_diagonal(i)
        more = i + 1 < nq
        scores(jnp.where(more, i + 1, 1), jnp.where(more, i + 1, 0))
        return 0

    lax.fori_loop(1, nq, diagonal, 0)

    def off_diagonal(_, carry):
        prev, cur = carry
        accumulate(*prev)
        softmax(cur[0])
        nxt = advance(*cur)
        scores(jnp.minimum(nxt[0], nq - 1), nxt[1])
        return cur, nxt

    start = ((jnp.int32(nq - 1), jnp.int32(nq - 1)), (jnp.int32(1), jnp.int32(0)))
    last, _ = lax.fori_loop(0, nq * (nq - 1) // 2, off_diagonal, start)
    accumulate(*last)

    def finish(i, _):
        heads = [acc_ref[hh, i][0:HEAD_DIM] / acc_ref[hh, i][HEAD_DIM:HEAD_DIM + 1] for hh in range(2)]
        o_ref[0, 0, rows_of(i), :] = jnp.concatenate(heads, axis=0).T.astype(o_ref.dtype)
        return 0

    lax.fori_loop(0, nq, finish, 0)


def _fox_attn(q, k, v, cum, pq, pk, qw2, kw2):
    b, pairs, s, _ = q.shape
    blk = ATTN_BLOCK
    seq = pl.BlockSpec((1, 1, s, LANES), lambda bi, pi: (bi, pi, 0, 0))
    place = pl.BlockSpec((1, LANES, LANES), lambda bi, pi: (pi, 0, 0))
    return pl.pallas_call(
        _fox_attn_kernel,
        out_shape=jax.ShapeDtypeStruct(q.shape, BF16),
        grid=(b, pairs),
        in_specs=[seq, seq, seq, pl.BlockSpec((1, s, LANES), lambda bi, pi: (bi, 0, 0)), place, place,
                  _resident(qw2.shape), _resident(kw2.shape)],
        out_specs=seq,
        scratch_shapes=[pltpu.VMEM((2, s, LANES), BF16), pltpu.VMEM((2, s, LANES), BF16),
                        pltpu.VMEM((2, s // blk, V_ROWS, blk), BF16), pltpu.VMEM((2, s // blk, 1, blk), F32),
                        pltpu.VMEM((2, s // blk, V_ROWS, blk), F32), pltpu.VMEM((2, blk, blk), F32),
                        pltpu.VMEM((2, blk, blk), BF16), pltpu.VMEM((2, 1, blk), F32),
                        pltpu.VMEM((2, 1, blk), F32)],
        compiler_params=_params("parallel", "parallel"),
        name="fox_attn",
    )(q, k, v, cum, pq, pk, qw2, kw2)


def _placement(pairs, lane0, heads):
    pq = np.zeros((pairs, LANES, LANES), np.float32)
    pk = np.zeros((pairs, LANES, LANES), np.float32)
    for p in range(pairs):
        for hh in range(2):
            e0 = HEAD_DIM * (1 - hh)
            for part in range(3):
                src = lane0 + part * heads + 2 * p + hh
                pq[p, src, e0 + part] = 1.0
                pq[p, 0, e0 + 3 + part] = 1.0
                pk[p, 0, e0 + part] = 1.0
                pk[p, src, e0 + 3 + part] = -1.0
    return jnp.asarray(pq, BF16), jnp.asarray(pk, BF16)


def _ssd_kernel(xbc_ref, z_ref, small_ref, shift_ref, cw_ref, cb_ref, dtb_ref, alog_ref, dskip_ref, nw_ref,
                expand_ref, o_ref, tail_ref, state_ref):
    n = SSD_CHUNK
    inner = z_ref.shape[2]
    gw = inner // SSD_GROUPS
    heads_per_group = gw // HEAD_DIM
    dot = functools.partial(jnp.dot, preferred_element_type=F32)
    pad = 8
    taps = CONV_WIDTH - 1

    @pl.when(pl.program_id(1) == 0)
    def _reset():
        state_ref[...] = jnp.zeros_like(state_ref)
        tail_ref[...] = jnp.zeros_like(tail_ref)

    u_bf = xbc_ref[0]
    u = u_bf.astype(F32)
    shifted = dot(shift_ref[...], u_bf)
    conv = cb_ref[...] + cw_ref[taps:taps + 1, :] * u
    hist = None
    for j in range(1, CONV_WIDTH):
        w = cw_ref[taps - j:taps - j + 1, :]
        conv = conv + w * shifted[(j - 1) * n:j * n]
        from_prev = w * tail_ref[pad - j:2 * pad - j, :]
        hist = from_prev if hist is None else hist + from_prev
    tail_ref[0:pad, :] = u[n - pad:n]
    conv = jnp.concatenate([conv[0:pad] + hist, conv[pad:]], axis=0)
    xc = _silu(conv)
    xs = xc[:, :inner]
    b_mat = xc[:, inner:inner + SSD_GROUPS * SSD_STATE].astype(BF16)
    c_mat = xc[:, inner + SSD_GROUPS * SSD_STATE:].astype(BF16)

    lane = lax.broadcasted_iota(jnp.int32, (n, LANES), 1)
    head_lane = lane < SSD_GROUPS * heads_per_group
    dt = jnp.where(head_lane, _softplus(small_ref[0] + dtb_ref[...]), 0.0)
    a_cs = _tri_cumsum(_lower_tri(n).astype(BF16), dt * (-jnp.exp(alog_ref[...])))
    a_cs_t = a_cs.T
    a_last = a_cs[n - 1:n, :]
    decay_in = jnp.exp(a_cs)
    decay_out = jnp.exp(a_last - a_cs)

    stacked = jnp.where(jnp.concatenate([head_lane] * 3, axis=0),
                        jnp.concatenate([dt, decay_in, decay_out], axis=0), 0.0)
    hi = stacked.astype(BF16).astype(F32)
    lo = pltpu.roll(stacked - hi, SSD_GROUPS * heads_per_group, axis=1)
    wide = dot((hi + lo).astype(BF16), expand_ref[...])
    dt_w, decay_in_w, decay_out_w = wide[:n], wide[n:2 * n], wide[2 * n:]
    chunk_decay_w = decay_in_w[n - 1:n, :]

    x_dt = xs * dt_w
    x_dt_bf = x_dt.astype(BF16)
    col = lax.broadcasted_iota(jnp.int32, (n, inner), 1)
    first_half = (col % LANES) < HEAD_DIM
    x_dt_even = jnp.where(first_half, x_dt_bf, jnp.zeros_like(x_dt_bf))
    x_dt_odd = jnp.where(first_half, jnp.zeros_like(x_dt_bf), x_dt_bf)
    x_out = (x_dt * decay_out_w).astype(BF16)

    tril = _lower_tri(n)
    ys = []
    for g in range(SSD_GROUPS):
        bg = b_mat[:, g * SSD_STATE:(g + 1) * SSD_STATE]
        cg = c_mat[:, g * SSD_STATE:(g + 1) * SSD_STATE]
        cb = lax.dot_general(cg, bg, (((1,), (1,)), ((), ())), preferred_element_type=F32)
        diag = []
        for hp in range(heads_per_group // 2):
            lanes = slice(g * gw + hp * LANES, g * gw + (hp + 1) * LANES)
            acc = None
            for sub, x_src in ((0, x_dt_even), (1, x_dt_odd)):
                h = g * heads_per_group + 2 * hp + sub
                seg = a_cs[:, h:h + 1] - a_cs_t[h:h + 1, :]
                mix = (cb * jnp.exp(jnp.where(tril, seg, NEG_BIG))).astype(BF16)
                part = dot(mix, x_src[:, lanes])
                acc = part if acc is None else acc + part
            diag.append(acc)
        cols = slice(g * gw, (g + 1) * gw)
        st = state_ref[g]
        y_off = dot(cg, st.astype(BF16)) * decay_in_w[:, cols]
        new = lax.dot_general(bg, x_out[:, cols], (((0,), (0,)), ((), ())), preferred_element_type=F32)
        state_ref[g] = chunk_decay_w[:, cols] * st + new
        ys.append(jnp.concatenate(diag, axis=1) + y_off)
    y = jnp.concatenate(ys, axis=1) + dskip_ref[...] * xs
    y = y * _silu(z_ref[0].astype(F32))
    normed = []
    for g in range(SSD_GROUPS):
        yg = y[:, g * gw:(g + 1) * gw]
        normed.append(yg * lax.rsqrt(jnp.mean(yg * yg, axis=-1, keepdims=True) + EPS))
    o_ref[0] = (jnp.concatenate(normed, axis=1) * nw_ref[...]).astype(o_ref.dtype)


def _ssd(xbc, z, small, cw, cb, dtb, alog, dskip, nw, expand):
    b, s, conv_dim = xbc.shape
    inner = z.shape[2]
    step = lambda n: pl.BlockSpec((1, SSD_CHUNK, n), lambda bi, ci: (bi, ci, 0))
    shift = jnp.concatenate([jnp.eye(SSD_CHUNK, k=-j, dtype=BF16) for j in range(1, CONV_WIDTH)], axis=0)
    consts = (shift, cw, cb, dtb, alog, dskip, nw, expand)
    return pl.pallas_call(
        _ssd_kernel,
        out_shape=jax.ShapeDtypeStruct((b, s, inner), BF16),
        grid=(b, s // SSD_CHUNK),
        in_specs=[step(conv_dim), step(inner), step(LANES)] + [_resident(c.shape) for c in consts],
        out_specs=step(inner),
        scratch_shapes=[pltpu.VMEM((16, conv_dim), F32),
                        pltpu.VMEM((SSD_GROUPS, SSD_STATE, inner // SSD_GROUPS), F32)],
        compiler_params=_params("arbitrary", "arbitrary"),
        name="ssd",
    )(xbc, z, small, *consts)


def _pad_lanes(v, offset=0):
    return jnp.zeros((1, LANES), F32).at[0, offset:offset + v.shape[0]].set(v.astype(F32))


def kernel(x, ffn1_norm, ffn1_w1, ffn1_w3, ffn1_w2, mix_norm, w_in, conv_w, conv_b, dt_bias, a_log, d_skip, ssd_norm_w, fox_b_f, q_norm_w, k_norm_w, w_branch_ssd, w_branch_fox, w_out, ffn2_norm, ffn2_w1, ffn2_w3, ffn2_w2):
    b, s, d = x.shape
    depth = ffn1_norm.shape[0]
    inner = w_branch_ssd.shape[1]
    fox_w = w_branch_fox.shape[1]
    ssd_heads = dt_bias.shape[1]
    fox_heads = fox_b_f.shape[1]
    conv_dim = conv_w.shape[-1]
    assert inner // HEAD_DIM == ssd_heads and fox_w // HEAD_DIM == fox_heads
    assert conv_dim == inner + 2 * SSD_GROUPS * SSD_STATE and conv_w.shape[1] == CONV_WIDTH
    assert (b * s) % TOKEN_TILE == 0 and s % ATTN_BLOCK == 0 and s // ATTN_BLOCK >= 2 and s % SSD_CHUNK == 0
    assert s % CUM_BLOCK == 0 and s % TOKEN_TILE == 0 and fox_w % LANES == 0
    sizes = (inner, conv_dim, ssd_heads, fox_w, fox_w, fox_w, fox_heads, d, d)
    assert sum(sizes) == w_in.shape[2] and 0 < ssd_heads and ssd_heads + 3 * fox_heads <= LANES
    offs = [sum(sizes[:i]) for i in range(len(sizes) + 1)]
    row = lambda v: v.astype(F32).reshape(1, -1)
    expand = ((jnp.arange(LANES)[:, None] % ssd_heads == (jnp.arange(inner) // HEAD_DIM)[None, :])
              & (jnp.arange(LANES)[:, None] < 2 * ssd_heads)).astype(BF16)

    x2d = x.reshape(b * s, d)
    for l in range(depth):
        x1 = _ffn(x2d, row(ffn1_norm[l]), ffn1_w1[l].astype(BF16), ffn1_w3[l].astype(BF16), ffn1_w2[l].astype(BF16))

        wl = w_in[l]
        seg = lambda i: wl[:, offs[i]:offs[i + 1]]
        w_small = jnp.zeros((d, LANES), F32).at[:, :ssd_heads].set(seg(2))
        for part in range(3):
            lo = ssd_heads + part * fox_heads
            w_small = w_small.at[:, lo:lo + fox_heads].set(seg(6))
        big = [seg(i).astype(BF16) for i in (0, 1, 3, 4, 5, 7, 8)]
        z, xbc, q, k, v, g_ssd, g_fox, small = _in_proj(
            x1, row(mix_norm[l]), big + [w_small.astype(BF16)], [BF16] * 7 + [F32],
            blocked=[False, False, True, True, True, False, False, False], batch=b)
        small3 = small.reshape(b, s, LANES)

        cum = _fox_cum(small3, _pad_lanes(jnp.tile(fox_b_f[l], 3), ssd_heads), ssd_heads, fox_heads)
        pq, pk = _placement(fox_w // LANES, ssd_heads, fox_heads)
        y_fox = _fox_attn(q, k, v, cum, pq, pk,
                          jnp.tile(row(q_norm_w[l]), (1, 2)), jnp.tile(row(k_norm_w[l]), (1, 2)))

        y_ssd = _ssd(xbc.reshape(b, s, conv_dim), z.reshape(b, s, inner), small3,
                     conv_w[l].reshape(CONV_WIDTH, conv_dim).astype(F32), row(conv_b[l]),
                     _pad_lanes(dt_bias[l]), _pad_lanes(a_log[l]),
                     jnp.repeat(row(d_skip[l]), HEAD_DIM, axis=1), row(ssd_norm_w[l]), expand)

        x2d = _merge_ffn(x1, g_ssd, g_fox, y_ssd.reshape(b * s, inner), y_fox,
                         w_branch_ssd[l].astype(BF16), w_branch_fox[l].astype(BF16), w_out[l].astype(BF16),
                         row(ffn2_norm[l]), ffn2_w1[l].astype(BF16), ffn2_w3[l].astype(BF16), ffn2_w2[l].astype(BF16))
    return x2d.reshape(b, s, d)
```

```python
import functools
import math

import numpy as np
import jax
import jax.numpy as jnp
from jax import lax
from jax.experimental import pallas as pl
from jax.experimental.pallas import tpu as pltpu

F32 = jnp.float32
BF16 = jnp.bfloat16

EPS = 1e-6
LANES = 128
HEAD_DIM = 64
SSD_GROUPS = 2
SSD_STATE = 128
CONV_WIDTH = 4
NEG_BIG = -1e30
LOG2E = math.log2(math.e)

TOKEN_TILE = 512
FF_CHUNK = 1024
ATTN_BLOCK = 512
V_ROWS = HEAD_DIM + 16
ATTN_ROWS = 2
SSD_CHUNK = 256
CUM_BLOCK = 256
VMEM_LIMIT = 56 * 1024 * 1024


def _params(*sem):
    return pltpu.CompilerParams(dimension_semantics=sem, vmem_limit_bytes=VMEM_LIMIT)


def _resident(shape):
    nd = len(shape)
    return pl.BlockSpec(shape, lambda *_: (0,) * nd, pipeline_mode=pl.Buffered(1))


def _rms(x, g):
    return x * lax.rsqrt(jnp.mean(x * x, axis=-1, keepdims=True) + EPS) * g


def _silu(x):
    return x * jax.nn.sigmoid(x)


def _softplus(x):
    return jnp.maximum(x, 0.0) + jnp.log1p(jnp.exp(-jnp.abs(x)))


def _split3(x):
    hi = x.astype(BF16).astype(F32)
    r = x - hi
    mid = r.astype(BF16).astype(F32)
    lo = (r - mid).astype(BF16).astype(F32)
    return hi, mid, lo


def _tri_cumsum(tri, x):
    hi, mid, lo = _split3(x)
    dot = functools.partial(jnp.dot, preferred_element_type=F32)
    return dot(tri, hi.astype(BF16)) + dot(tri, mid.astype(BF16)) + dot(tri, lo.astype(BF16))


def _lower_tri(n):
    r = lax.broadcasted_iota(jnp.int32, (n, n), 0)
    c = lax.broadcasted_iota(jnp.int32, (n, n), 1)
    return r >= c


def _swiglu_residual(x, g, w1_ref, w3_ref, w2_ref):
    h = _rms(x, g).astype(BF16)
    d_ff = w1_ref.shape[1]
    acc = None
    for c0 in range(0, d_ff, FF_CHUNK):
        c1 = min(c0 + FF_CHUNK, d_ff)
        a = jnp.dot(h, w1_ref[:, c0:c1], preferred_element_type=F32)
        b = jnp.dot(h, w3_ref[:, c0:c1], preferred_element_type=F32)
        gate = (_silu(a) * b).astype(BF16)
        part = jnp.dot(gate, w2_ref[c0:c1, :], preferred_element_type=F32)
        acc = part if acc is None else acc + part
    return x + 0.5 * acc


def _ffn_kernel(x_ref, g_ref, w1_ref, w3_ref, w2_ref, o_ref):
    o_ref[...] = _swiglu_residual(x_ref[...], g_ref[...], w1_ref, w3_ref, w2_ref)


def _ffn(x2d, g, w1, w3, w2):
    t, d = x2d.shape
    tile = pl.BlockSpec((TOKEN_TILE, d), lambda i: (i, 0))
    return pl.pallas_call(
        _ffn_kernel,
        out_shape=jax.ShapeDtypeStruct((t, d), F32),
        grid=(t // TOKEN_TILE,),
        in_specs=[tile, _resident(g.shape), _resident(w1.shape), _resident(w3.shape), _resident(w2.shape)],
        out_specs=tile,
        compiler_params=_params("parallel"),
        name="ffn1",
    )(x2d, g, w1, w3, w2)


def _in_proj_kernel(x_ref, g_ref, *refs):
    n = len(refs) // 2
    h = _rms(x_ref[...], g_ref[...]).astype(BF16)
    for w_ref, o_ref in zip(refs[:n], refs[n:]):
        res = jnp.dot(h, w_ref[...], preferred_element_type=F32).astype(o_ref.dtype)
        if len(o_ref.shape) == 2:
            o_ref[...] = res
        else:
            for p in range(o_ref.shape[1]):
                o_ref[0, p] = res[:, p * LANES:(p + 1) * LANES]


def _column_block_spec(tiles_per_seq, blocks):
    return pl.BlockSpec((1, blocks, TOKEN_TILE, LANES), lambda i: (i // tiles_per_seq, 0, i % tiles_per_seq, 0))


def _in_proj(x2d, g, weights, out_dtypes, blocked, batch):
    t, d = x2d.shape
    seq = t // batch
    tile = lambda n: pl.BlockSpec((TOKEN_TILE, n), lambda i: (i, 0))
    shapes, specs = [], []
    for w, dt, blk in zip(weights, out_dtypes, blocked):
        n = w.shape[1]
        shapes.append(jax.ShapeDtypeStruct((batch, n // LANES, seq, LANES) if blk else (t, n), dt))
        specs.append(_column_block_spec(seq // TOKEN_TILE, n // LANES) if blk else tile(n))
    return pl.pallas_call(
        _in_proj_kernel,
        out_shape=shapes,
        grid=(t // TOKEN_TILE,),
        in_specs=[tile(d), _resident(g.shape)] + [_resident(w.shape) for w in weights],
        out_specs=specs,
        compiler_params=_params("parallel"),
        name="in_proj",
    )(x2d, g, *weights)


def _merge_ffn_kernel(x_ref, gs_ref, gf_ref, ys_ref, yf_ref, wbs_ref, wbf_ref, wo_ref,
                      g_ref, w1_ref, w3_ref, w2_ref, o_ref):
    dot = functools.partial(jnp.dot, preferred_element_type=F32)
    y_fox = jnp.concatenate([yf_ref[0, p] for p in range(yf_ref.shape[1])], axis=1)
    merged = (jax.nn.sigmoid(gs_ref[...].astype(F32)) * dot(ys_ref[...], wbs_ref[...])
              + jax.nn.sigmoid(gf_ref[...].astype(F32)) * dot(y_fox, wbf_ref[...]))
    x2 = x_ref[...] + dot(merged.astype(BF16), wo_ref[...])
    o_ref[...] = _swiglu_residual(x2, g_ref[...], w1_ref, w3_ref, w2_ref)


def _merge_ffn(x2d, gs, gf, ys, yf, wbs, wbf, wo, g, w1, w3, w2):
    t, d = x2d.shape
    tile = pl.BlockSpec((TOKEN_TILE, d), lambda i: (i, 0))
    consts = (wbs, wbf, wo, g, w1, w3, w2)
    return pl.pallas_call(
        _merge_ffn_kernel,
        out_shape=jax.ShapeDtypeStruct((t, d), F32),
        grid=(t // TOKEN_TILE,),
        in_specs=[tile] * 4 + [_column_block_spec(yf.shape[2] // TOKEN_TILE, yf.shape[1])]
                 + [_resident(c.shape) for c in consts],
        out_specs=tile,
        compiler_params=_params("parallel"),
        name="merge_ffn2",
    )(x2d, gs, gf, ys, yf, *consts)


def _fox_cum_kernel(small_ref, bias_ref, o_ref, *, lane0, heads):
    s = small_ref.shape[1]
    tri = _lower_tri(CUM_BLOCK).astype(BF16)
    bias = bias_ref[...]
    lane = lax.broadcasted_iota(jnp.int32, (CUM_BLOCK, LANES), 1)

    def step(i, carry):
        rows = pl.ds(pl.multiple_of(i * CUM_BLOCK, CUM_BLOCK), CUM_BLOCK)
        log_f = -_softplus(-(small_ref[0, rows, :] + bias))
        c = _tri_cumsum(tri, log_f) + carry
        hi, mid, lo = _split3(c * LOG2E)
        parts = jnp.where(lane < lane0 + heads, hi, jnp.where(lane < lane0 + 2 * heads, mid, lo))
        o_ref[0, rows, :] = jnp.where(lane == 0, 1.0, parts).astype(o_ref.dtype)
        return c[CUM_BLOCK - 1:CUM_BLOCK, :]

    lax.fori_loop(0, s // CUM_BLOCK, step, jnp.zeros((1, LANES), F32))


def _fox_cum(small, bias, lane0, heads):
    b, s, _ = small.shape
    blk = pl.BlockSpec((1, s, LANES), lambda i: (i, 0, 0))
    return pl.pallas_call(
        functools.partial(_fox_cum_kernel, lane0=lane0, heads=heads),
        out_shape=jax.ShapeDtypeStruct((b, s, LANES), BF16),
        grid=(b,),
        in_specs=[blk, _resident(bias.shape)],
        out_specs=blk,
        compiler_params=_params("parallel"),
        name="fox_cum",
    )(small, bias)


def _fox_attn_kernel(q_ref, k_ref, v_ref, cum_ref, pq_ref, pk_ref, qw_ref, kw_ref, o_ref,
                     qa_ref, ka_ref, vt_ref, m_ref, acc_ref, s_ref, p_ref, alpha_ref, smax_ref):
    s = q_ref.shape[2]
    blk = ATTN_BLOCK
    nq = s // blk
    heads = 2 * ATTN_ROWS
    rows_of = lambda i: pl.ds(pl.multiple_of(i * blk, blk), blk)
    dot = functools.partial(jnp.dot, preferred_element_type=F32)
    dot_nt = lambda a, b: lax.dot_general(a, b, (((1,), (1,)), ((), ())), preferred_element_type=F32)

    qw = qw_ref[...] * (HEAD_DIM ** -0.5 * LOG2E)
    kw = kw_ref[...]

    def prepare(i, _):
        r = rows_of(i)
        low = lax.broadcasted_iota(jnp.int32, (blk, LANES), 1) < HEAD_DIM
        same_head = ((lax.broadcasted_iota(jnp.int32, (LANES, LANES), 0) < HEAD_DIM)
                     == (lax.broadcasted_iota(jnp.int32, (LANES, LANES), 1) < HEAD_DIM)).astype(BF16)

        def head_norm(x, w):
            mean_sq = dot((x * x).astype(BF16), same_head) * (1.0 / HEAD_DIM)
            return x * lax.rsqrt(mean_sq + EPS) * w

        ones = jnp.ones((V_ROWS - HEAD_DIM, blk), F32)
        for bb in range(ATTN_ROWS):
            c3 = cum_ref[bb, r, :]
            qn, q_ext = head_norm(q_ref[bb, 0, r, :].astype(F32), qw), dot(c3, pq_ref[0])
            kn, k_ext = head_norm(k_ref[bb, 0, r, :].astype(F32), kw), dot(c3, pk_ref[0])
            qa_ref[2 * bb, r, :] = jnp.where(low, qn, q_ext).astype(BF16)
            qa_ref[2 * bb + 1, r, :] = jnp.where(low, q_ext, qn).astype(BF16)
            ka_ref[2 * bb, r, :] = jnp.where(low, kn, k_ext).astype(BF16)
            ka_ref[2 * bb + 1, r, :] = jnp.where(low, k_ext, kn).astype(BF16)
            v_t = v_ref[bb, 0, r, :].astype(F32).T
            for hh in range(2):
                vt_ref[2 * bb + hh, i] = jnp.concatenate(
                    [v_t[hh * HEAD_DIM:(hh + 1) * HEAD_DIM], ones], axis=0).astype(BF16)
        return 0

    lax.fori_loop(0, nq, prepare, 0)

    def scores(qb, kb):
        for hh in range(heads):
            sc = dot_nt(ka_ref[hh, rows_of(kb), :], qa_ref[hh, rows_of(qb), :])
            s_ref[hh] = sc
            smax_ref[hh] = jnp.max(sc, axis=0, keepdims=True)

    def softmax_diagonal(qb):
        causal = (lax.broadcasted_iota(jnp.int32, (blk, blk), 0) <= lax.broadcasted_iota(jnp.int32, (blk, blk), 1))
        for hh in range(heads):
            sc = jnp.where(causal, s_ref[hh], NEG_BIG)
            m = jnp.max(sc, axis=0, keepdims=True)
            p_ref[hh] = jnp.exp2(sc - m).astype(BF16)
            m_ref[hh, qb] = m

    def softmax(qb):
        for hh in range(heads):
            sc = s_ref[hh]
            m_old = m_ref[hh, qb]
            m_new = jnp.maximum(m_old, smax_ref[hh])
            p_ref[hh] = jnp.exp2(sc - m_new).astype(BF16)
            alpha_ref[hh] = jnp.exp2(m_old - m_new)
            m_ref[hh, qb] = m_new

    def accumulate(qb, kb):
        for hh in range(heads):
            acc_ref[hh, qb] = alpha_ref[hh] * acc_ref[hh, qb] + dot(vt_ref[hh, kb], p_ref[hh])

    def advance(qb, kb):
        wrap = kb + 1 >= qb
        return jnp.where(wrap, qb + 1, qb), jnp.where(wrap, 0, kb + 1)

    acc_ref[...] = jnp.zeros_like(acc_ref)
    alpha_ref[...] = jnp.ones_like(alpha_ref)
    scores(0, 0)
    softmax_diagonal(0)
    scores(1, 1)

    def diagonal(i, _):
        accumulate(i - 1, i - 1)
        softmax_diagonal(i)
        more = i + 1 < nq
        scores(jnp.where(more, i + 1, 1), jnp.where(more, i + 1, 0))
        return 0

    lax.fori_loop(1, nq, diagonal, 0)

    def off_diagonal(_, carry):
        prev, cur = carry
        accumulate(*prev)
        softmax(cur[0])
        nxt = advance(*cur)
        scores(jnp.minimum(nxt[0], nq - 1), nxt[1])
        return cur, nxt

    start = ((jnp.int32(nq - 1), jnp.int32(nq - 1)), (jnp.int32(1), jnp.int32(0)))
    last, _ = lax.fori_loop(0, nq * (nq - 1) // 2, off_diagonal, start)
    accumulate(*last)

    def finish(i, _):
        for bb in range(ATTN_ROWS):
            pair = [acc_ref[2 * bb + hh, i][0:HEAD_DIM] / acc_ref[2 * bb + hh, i][HEAD_DIM:HEAD_DIM + 1]
                    for hh in range(2)]
            o_ref[bb, 0, rows_of(i), :] = jnp.concatenate(pair, axis=0).T.astype(o_ref.dtype)
        return 0

    lax.fori_loop(0, nq, finish, 0)


def _fox_attn(q, k, v, cum, pq, pk, qw2, kw2):
    b, pairs, s, _ = q.shape
    blk = ATTN_BLOCK
    rows, heads = ATTN_ROWS, 2 * ATTN_ROWS
    seq = pl.BlockSpec((rows, 1, s, LANES), lambda bi, pi: (bi, pi, 0, 0))
    place = pl.BlockSpec((1, LANES, LANES), lambda bi, pi: (pi, 0, 0))
    return pl.pallas_call(
        _fox_attn_kernel,
        out_shape=jax.ShapeDtypeStruct(q.shape, BF16),
        grid=(b // rows, pairs),
        in_specs=[seq, seq, seq, pl.BlockSpec((rows, s, LANES), lambda bi, pi: (bi, 0, 0)), place, place,
                  _resident(qw2.shape), _resident(kw2.shape)],
        out_specs=seq,
        scratch_shapes=[pltpu.VMEM((heads, s, LANES), BF16), pltpu.VMEM((heads, s, LANES), BF16),
                        pltpu.VMEM((heads, s // blk, V_ROWS, blk), BF16), pltpu.VMEM((heads, s // blk, 1, blk), F32),
                        pltpu.VMEM((heads, s // blk, V_ROWS, blk), F32), pltpu.VMEM((heads, blk, blk), F32),
                        pltpu.VMEM((heads, blk, blk), BF16), pltpu.VMEM((heads, 1, blk), F32),
                        pltpu.VMEM((heads, 1, blk), F32)],
        compiler_params=_params("parallel", "parallel"),
        name="fox_attn",
    )(q, k, v, cum, pq, pk, qw2, kw2)


def _placement(pairs, lane0, heads):
    pq = np.zeros((pairs, LANES, LANES), np.float32)
    pk = np.zeros((pairs, LANES, LANES), np.float32)
    for p in range(pairs):
        for hh in range(2):
            e0 = HEAD_DIM * (1 - hh)
            for part in range(3):
                src = lane0 + part * heads + 2 * p + hh
                pq[p, src, e0 + part] = 1.0
                pq[p, 0, e0 + 3 + part] = 1.0
                pk[p, 0, e0 + part] = 1.0
                pk[p, src, e0 + 3 + part] = -1.0
    return jnp.asarray(pq, BF16), jnp.asarray(pk, BF16)


def _ssd_kernel(xbc_ref, z_ref, small_ref, shift_ref, cw_ref, cb_ref, dtb_ref, alog_ref, dskip_ref, nw_ref,
                expand_ref, o_ref, tail_ref, state_ref):
    n = SSD_CHUNK
    inner = z_ref.shape[2]
    gw = inner // SSD_GROUPS
    heads_per_group = gw // HEAD_DIM
    dot = functools.partial(jnp.dot, preferred_element_type=F32)
    pad = 8
    taps = CONV_WIDTH - 1

    @pl.when(pl.program_id(1) == 0)
    def _reset():
        state_ref[...] = jnp.zeros_like(state_ref)
        tail_ref[...] = jnp.zeros_like(tail_ref)

    u_bf = xbc_ref[0]
    u = u_bf.astype(F32)
    shifted = dot(shift_ref[...], u_bf)
    conv = cb_ref[...] + cw_ref[taps:taps + 1, :] * u
    hist = None
    for j in range(1, CONV_WIDTH):
        w = cw_ref[taps - j:taps - j + 1, :]
        conv = conv + w * shifted[(j - 1) * n:j * n]
        from_prev = w * tail_ref[pad - j:2 * pad - j, :]
        hist = from_prev if hist is None else hist + from_prev
    tail_ref[0:pad, :] = u[n - pad:n]
    conv = jnp.concatenate([conv[0:pad] + hist, conv[pad:]], axis=0)
    xc = _silu(conv)
    xs = xc[:, :inner]
    b_mat = xc[:, inner:inner + SSD_GROUPS * SSD_STATE].astype(BF16)
    c_mat = xc[:, inner + SSD_GROUPS * SSD_STATE:].astype(BF16)

    lane = lax.broadcasted_iota(jnp.int32, (n, LANES), 1)
    head_lane = lane < SSD_GROUPS * heads_per_group
    dt = jnp.where(head_lane, _softplus(small_ref[0] + dtb_ref[...]), 0.0)
    a_cs = _tri_cumsum(_lower_tri(n).astype(BF16), dt * (-jnp.exp(alog_ref[...])))
    a_cs_t = a_cs.T
    a_last = a_cs[n - 1:n, :]
    decay_in = jnp.exp(a_cs)
    decay_out = jnp.exp(a_last - a_cs)

    stacked = jnp.where(jnp.concatenate([head_lane] * 3, axis=0),
                        jnp.concatenate([dt, decay_in, decay_out], axis=0), 0.0)
    hi = stacked.astype(BF16).astype(F32)
    lo = pltpu.roll(stacked - hi, SSD_GROUPS * heads_per_group, axis=1)
    wide = dot((hi + lo).astype(BF16), expand_ref[...])
    dt_w, decay_in_w, decay_out_w = wide[:n], wide[n:2 * n], wide[2 * n:]
    chunk_decay_w = decay_in_w[n - 1:n, :]

    x_dt = xs * dt_w
    x_dt_bf = x_dt.astype(BF16)
    col = lax.broadcasted_iota(jnp.int32, (n, inner), 1)
    first_half = (col % LANES) < HEAD_DIM
    x_dt_even = jnp.where(first_half, x_dt_bf, jnp.zeros_like(x_dt_bf))
    x_dt_odd = jnp.where(first_half, jnp.zeros_like(x_dt_bf), x_dt_bf)
    x_out = (x_dt * decay_out_w).astype(BF16)

    tril = _lower_tri(n)
    ys = []
    for g in range(SSD_GROUPS):
        bg = b_mat[:, g * SSD_STATE:(g + 1) * SSD_STATE]
        cg = c_mat[:, g * SSD_STATE:(g + 1) * SSD_STATE]
        cb = lax.dot_general(cg, bg, (((1,), (1,)), ((), ())), preferred_element_type=F32)
        diag = []
        for hp in range(heads_per_group // 2):
            lanes = slice(g * gw + hp * LANES, g * gw + (hp + 1) * LANES)
            acc = None
            for sub, x_src in ((0, x_dt_even), (1, x_dt_odd)):
                h = g * heads_per_group + 2 * hp + sub
                seg = a_cs[:, h:h + 1] - a_cs_t[h:h + 1, :]
                mix = (cb * jnp.exp(jnp.where(tril, seg, NEG_BIG))).astype(BF16)
                part = dot(mix, x_src[:, lanes])
                acc = part if acc is None else acc + part
            diag.append(acc)
        cols = slice(g * gw, (g + 1) * gw)
        st = state_ref[g]
        y_off = dot(cg, st.astype(BF16)) * decay_in_w[:, cols]
        new = lax.dot_general(bg, x_out[:, cols], (((0,), (0,)), ((), ())), preferred_element_type=F32)
        state_ref[g] = chunk_decay_w[:, cols] * st + new
        ys.append(jnp.concatenate(diag, axis=1) + y_off)
    y = jnp.concatenate(ys, axis=1) + dskip_ref[...] * xs
    y = y * _silu(z_ref[0].astype(F32))
    normed = []
    for g in range(SSD_GROUPS):
        yg = y[:, g * gw:(g + 1) * gw]
        normed.append(yg * lax.rsqrt(jnp.mean(yg * yg, axis=-1, keepdims=True) + EPS))
    o_ref[0] = (jnp.concatenate(normed, axis=1) * nw_ref[...]).astype(o_ref.dtype)


def _ssd(xbc, z, small, cw, cb, dtb, alog, dskip, nw, expand):
    b, s, conv_dim = xbc.shape
    inner = z.shape[2]
    step = lambda n: pl.BlockSpec((1, SSD_CHUNK, n), lambda bi, ci: (bi, ci, 0))
    shift = jnp.concatenate([jnp.eye(SSD_CHUNK, k=-j, dtype=BF16) for j in range(1, CONV_WIDTH)], axis=0)
    consts = (shift, cw, cb, dtb, alog, dskip, nw, expand)
    return pl.pallas_call(
        _ssd_kernel,
        out_shape=jax.ShapeDtypeStruct((b, s, inner), BF16),
        grid=(b, s // SSD_CHUNK),
        in_specs=[step(conv_dim), step(inner), step(LANES)] + [_resident(c.shape) for c in consts],
        out_specs=step(inner),
        scratch_shapes=[pltpu.VMEM((16, conv_dim), F32),
                        pltpu.VMEM((SSD_GROUPS, SSD_STATE, inner // SSD_GROUPS), F32)],
        compiler_params=_params("arbitrary", "arbitrary"),
        name="ssd",
    )(xbc, z, small, *consts)


def _pad_lanes(v, offset=0):
    return jnp.zeros((1, LANES), F32).at[0, offset:offset + v.shape[0]].set(v.astype(F32))


def kernel(x, ffn1_norm, ffn1_w1, ffn1_w3, ffn1_w2, mix_norm, w_in, conv_w, conv_b, dt_bias, a_log, d_skip, ssd_norm_w, fox_b_f, q_norm_w, k_norm_w, w_branch_ssd, w_branch_fox, w_out, ffn2_norm, ffn2_w1, ffn2_w3, ffn2_w2):
    b, s, d = x.shape
    depth = ffn1_norm.shape[0]
    inner = w_branch_ssd.shape[1]
    fox_w = w_branch_fox.shape[1]
    ssd_heads = dt_bias.shape[1]
    fox_heads = fox_b_f.shape[1]
    conv_dim = conv_w.shape[-1]
    assert inner // HEAD_DIM == ssd_heads and fox_w // HEAD_DIM == fox_heads
    assert conv_dim == inner + 2 * SSD_GROUPS * SSD_STATE and conv_w.shape[1] == CONV_WIDTH
    assert (b * s) % TOKEN_TILE == 0 and s % ATTN_BLOCK == 0 and s // ATTN_BLOCK >= 2 and s % SSD_CHUNK == 0
    assert s % CUM_BLOCK == 0 and s % TOKEN_TILE == 0 and fox_w % LANES == 0
    sizes = (inner, conv_dim, ssd_heads, fox_w, fox_w, fox_w, fox_heads, d, d)
    assert sum(sizes) == w_in.shape[2] and 0 < ssd_heads and ssd_heads + 3 * fox_heads <= LANES
    offs = [sum(sizes[:i]) for i in range(len(sizes) + 1)]
    row = lambda v: v.astype(F32).reshape(1, -1)
    expand = ((jnp.arange(LANES)[:, None] % ssd_heads == (jnp.arange(inner) // HEAD_DIM)[None, :])
              & (jnp.arange(LANES)[:, None] < 2 * ssd_heads)).astype(BF16)

    x2d = x.reshape(b * s, d)
    for l in range(depth):
        x1 = _ffn(x2d, row(ffn1_norm[l]), ffn1_w1[l].astype(BF16), ffn1_w3[l].astype(BF16), ffn1_w2[l].astype(BF16))

        wl = w_in[l]
        seg = lambda i: wl[:, offs[i]:offs[i + 1]]
        w_small = jnp.zeros((d, LANES), F32).at[:, :ssd_heads].set(seg(2))
        for part in range(3):
            lo = ssd_heads + part * fox_heads
            w_small = w_small.at[:, lo:lo + fox_heads].set(seg(6))
        big = [seg(i).astype(BF16) for i in (0, 1, 3, 4, 5, 7, 8)]
        z, xbc, q, k, v, g_ssd, g_fox, small = _in_proj(
            x1, row(mix_norm[l]), big + [w_small.astype(BF16)], [BF16] * 7 + [F32],
            blocked=[False, False, True, True, True, False, False, False], batch=b)
        small3 = small.reshape(b, s, LANES)

        cum = _fox_cum(small3, _pad_lanes(jnp.tile(fox_b_f[l], 3), ssd_heads), ssd_heads, fox_heads)
        pq, pk = _placement(fox_w // LANES, ssd_heads, fox_heads)
        y_fox = _fox_attn(q, k, v, cum, pq, pk,
                          jnp.tile(row(q_norm_w[l]), (1, 2)), jnp.tile(row(k_norm_w[l]), (1, 2)))

        y_ssd = _ssd(xbc.reshape(b, s, conv_dim), z.reshape(b, s, inner), small3,
                     conv_w[l].reshape(CONV_WIDTH, conv_dim).astype(F32), row(conv_b[l]),
                     _pad_lanes(dt_bias[l]), _pad_lanes(a_log[l]),
                     jnp.repeat(row(d_skip[l]), HEAD_DIM, axis=1), row(ssd_norm_w[l]), expand)

        x2d = _merge_ffn(x1, g_ssd, g_fox, y_ssd.reshape(b * s, inner), y_fox,
                         w_branch_ssd[l].astype(BF16), w_branch_fox[l].astype(BF16), w_out[l].astype(BF16),
                         row(ffn2_norm[l]), ffn2_w1[l].astype(BF16), ffn2_w3[l].astype(BF16), ffn2_w2[l].astype(BF16))
    return x2d.reshape(b, s, d)
```
